```python
import math
import jax, jax.numpy as jnp
from jax import lax
import numpy as np

D_MODEL = 1024
BATCH = 8
SEQ = 2048
DEPTH = 4

HEAD_DIM = 64
FOX_HEADS = 6
DIFF_HEADS = 4
DIFF_QK_DIM = 32
DIFF_V_DIM = 64
DSA_HEADS = 6
DSA_KV_DIM = 64
IDX_HEADS = 4
IDX_DIM = 64
DSA_TOPK = 256
ROPE_THETA = 500000.0
ROPE_FRACTION = 4
Q_BLOCK = 128
NORM_EPS = 1e-6
FFN_HIDDEN = -(-8 * D_MODEL // (3 * 256)) * 256

MIX_WIDTH = FOX_HEADS * HEAD_DIM + DIFF_HEADS * DIFF_V_DIM + DSA_HEADS * HEAD_DIM

SPLIT_SIZES = (
    FOX_HEADS * HEAD_DIM,
    FOX_HEADS * HEAD_DIM,
    FOX_HEADS * HEAD_DIM,
    FOX_HEADS,
    DIFF_HEADS * 2 * DIFF_QK_DIM,
    DIFF_HEADS * 2 * DIFF_QK_DIM,
    DIFF_HEADS * DIFF_V_DIM,
    DSA_HEADS * HEAD_DIM,
    DSA_KV_DIM,
    DSA_KV_DIM,
    IDX_HEADS * IDX_DIM,
    IDX_DIM,
    IDX_HEADS,
)
IN_WIDTH = sum(SPLIT_SIZES)

kernel_name = "hybrid_fox_diff_dsa_trunk"


def rms_norm(x, g):
    xf = x.astype(jnp.float32)
    y = xf * lax.rsqrt(jnp.mean(xf * xf, axis=-1, keepdims=True) + NORM_EPS)
    return (y * g.astype(jnp.float32)).astype(x.dtype)


def rope_tables(seq_len, head_dim):
    rot = head_dim // ROPE_FRACTION
    inv_freq = 1.0 / (ROPE_THETA ** (jnp.arange(0, rot, 2, dtype=jnp.float32) / rot))
    ang = jnp.arange(seq_len, dtype=jnp.float32)[:, None] * inv_freq[None, :]
    return jnp.cos(ang), jnp.sin(ang)


def partial_rope(x, cos, sin):
    rot = x.shape[-1] // ROPE_FRACTION
    half = rot // 2
    c = cos.astype(x.dtype)
    s = sin.astype(x.dtype)
    x1 = x[..., :half]
    x2 = x[..., half:rot]
    return jnp.concatenate([x1 * c - x2 * s, x2 * c + x1 * s, x[..., rot:]], axis=-1)


def blocks_to_seq(out):
    out = jnp.moveaxis(out, 0, 1)
    return out.reshape((out.shape[0], out.shape[1] * out.shape[2]) + out.shape[3:])


def causal_mask(start, seq_len):
    qpos = start + jnp.arange(Q_BLOCK)
    kpos = jnp.arange(seq_len)
    return kpos[None, :] <= qpos[:, None], qpos


def fox_attention(q, k, v, cum_logf):
    S = q.shape[2]
    scale = q.shape[-1] ** -0.5

    def block(i):
        start = i * Q_BLOCK
        qb = lax.dynamic_slice_in_dim(q, start, Q_BLOCK, axis=2)
        cb = lax.dynamic_slice_in_dim(cum_logf, start, Q_BLOCK, axis=2)
        logits = (jnp.einsum('bhqd,bhkd->bhqk', qb, k).astype(jnp.float32) * scale
                  + cb[..., :, None] - cum_logf[..., None, :])
        mask, _ = causal_mask(start, S)
        p = jax.nn.softmax(jnp.where(mask, logits, -jnp.inf), axis=-1).astype(v.dtype)
        return jnp.einsum('bhqk,bhkd->bqhd', p, v)

    return blocks_to_seq(lax.map(block, jnp.arange(S // Q_BLOCK)))


def diff_attention(q, k, v, lam):
    S = q.shape[3]
    scale = q.shape[-1] ** -0.5

    def block(i):
        start = i * Q_BLOCK
        qb = lax.dynamic_slice_in_dim(q, start, Q_BLOCK, axis=3)
        logits = jnp.einsum('bhmqd,bhmkd->bhmqk', qb, k).astype(jnp.float32) * scale
        mask, _ = causal_mask(start, S)
        p = jax.nn.softmax(jnp.where(mask, logits, -jnp.inf), axis=-1)
        a = (p[:, :, 0] - lam * p[:, :, 1]).astype(v.dtype)
        return jnp.einsum('bhqk,bhkd->bqhd', a, v)

    return blocks_to_seq(lax.map(block, jnp.arange(S // Q_BLOCK)))


def dsa_attention(q, k, v, q_idx, k_idx, w_idx, topk):
    S = q.shape[2]
    scale = q.shape[-1] ** -0.5
    idx_scale = q_idx.shape[-1] ** -0.5
    gather = jax.vmap(lambda table, ix: table[ix])

    def block(i):
        start = i * Q_BLOCK
        qib = lax.dynamic_slice_in_dim(q_idx, start, Q_BLOCK, axis=2)
        wib = lax.dynamic_slice_in_dim(w_idx, start, Q_BLOCK, axis=1).astype(jnp.float32)
        s_idx = jnp.einsum('bhqd,bkd->bhqk', qib, k_idx).astype(jnp.float32) * idx_scale
        score = jnp.einsum('bqh,bhqk->bqk', wib, jax.nn.relu(s_idx))
        mask, qpos = causal_mask(start, S)
        score = jnp.where(mask[None], score, -jnp.inf)
        _, sel = lax.top_k(score, topk)
        valid = sel <= qpos[None, :, None]
        kg = gather(k, sel)
        vg = gather(v, sel)
        qb = lax.dynamic_slice_in_dim(q, start, Q_BLOCK, axis=2)
        logits = jnp.einsum('bhqd,bqjd->bhqj', qb, kg).astype(jnp.float32) * scale
        p = jax.nn.softmax(jnp.where(valid[:, None], logits, -jnp.inf), axis=-1).astype(vg.dtype)
        return jnp.einsum('bhqj,bqjd->bqhd', p, vg)

    return blocks_to_seq(lax.map(block, jnp.arange(S // Q_BLOCK)))


def setup_inputs(seed: int = 0) -> dict:
    key = jax.random.key(seed)
    ks = jax.random.split(key, 19)
    f32 = jnp.float32

    def nrm(k, shape, scale):
        return jax.random.normal(k, shape, f32) * scale

    res_scale = 1.0 / math.sqrt(2 * DEPTH)
    return {
        "x": nrm(ks[0], (BATCH, SEQ, D_MODEL), 1.0),
        "attn_norm": 1.0 + nrm(ks[1], (DEPTH, D_MODEL), 0.02),
        "w_in": nrm(ks[2], (DEPTH, D_MODEL, IN_WIDTH), D_MODEL ** -0.5),
        "fox_fb": jax.random.uniform(ks[3], (DEPTH, FOX_HEADS), f32, 1.0, 4.0),
        "fox_qn": 1.0 + nrm(ks[4], (DEPTH, HEAD_DIM), 0.02),
        "fox_kn": 1.0 + nrm(ks[5], (DEPTH, HEAD_DIM), 0.02),
        "diff_qn": 1.0 + nrm(ks[6], (DEPTH, DIFF_QK_DIM), 0.02),
        "diff_kn": 1.0 + nrm(ks[7], (DEPTH, DIFF_QK_DIM), 0.02),
        "diff_lq1": nrm(ks[8], (DEPTH, DIFF_QK_DIM), 0.1),
        "diff_lk1": nrm(ks[9], (DEPTH, DIFF_QK_DIM), 0.1),
        "diff_lq2": nrm(ks[10], (DEPTH, DIFF_QK_DIM), 0.1),
        "diff_lk2": nrm(ks[11], (DEPTH, DIFF_QK_DIM), 0.1),
        "diff_subln": 1.0 + nrm(ks[12], (DEPTH, DIFF_V_DIM), 0.02),
        "dsa_qn": 1.0 + nrm(ks[13], (DEPTH, HEAD_DIM), 0.02),
        "dsa_kn": 1.0 + nrm(ks[14], (DEPTH, DSA_KV_DIM), 0.02),
        "w_out": nrm(ks[15], (DEPTH, MIX_WIDTH, D_MODEL), MIX_WIDTH ** -0.5 * res_scale),
        "ffn_norm": 1.0 + nrm(ks[16], (DEPTH, D_MODEL), 0.02),
        "w_gate_up": nrm(ks[17], (DEPTH, D_MODEL, 2 * FFN_HIDDEN), D_MODEL ** -0.5),
        "w_down": nrm(ks[18], (DEPTH, FFN_HIDDEN, D_MODEL), FFN_HIDDEN ** -0.5 * res_scale),
    }


def reference(x, attn_norm, w_in, fox_fb, fox_qn, fox_kn, diff_qn, diff_kn,
              diff_lq1, diff_lk1, diff_lq2, diff_lk2, diff_subln, dsa_qn, dsa_kn,
              w_out, ffn_norm, w_gate_up, w_down):
    B, S, _ = x.shape
    topk = min(DSA_TOPK, S // 4)
    cos64, sin64 = rope_tables(S, HEAD_DIM)
    cos32, sin32 = rope_tables(S, DIFF_QK_DIM)
    split_points = [int(v) for v in np.cumsum(SPLIT_SIZES)[:-1]]

    for l in range(DEPTH):
        h = rms_norm(x, attn_norm[l])
        proj = h @ w_in[l]
        (fq, fk, fv, ff, dq, dk, dv, sq, sk, sv, iq, ik, iw) = jnp.split(proj, split_points, axis=-1)

        fq = rms_norm(fq.reshape(B, S, FOX_HEADS, HEAD_DIM), fox_qn[l]).transpose(0, 2, 1, 3)
        fk = rms_norm(fk.reshape(B, S, FOX_HEADS, HEAD_DIM), fox_kn[l]).transpose(0, 2, 1, 3)
        fv = fv.reshape(B, S, FOX_HEADS, HEAD_DIM).transpose(0, 2, 1, 3)
        log_f = jax.nn.log_sigmoid(ff.astype(jnp.float32) + fox_fb[l].astype(jnp.float32))
        cum_logf = jnp.cumsum(log_f, axis=1).transpose(0, 2, 1)
        o_fox = fox_attention(fq, fk, fv, cum_logf).reshape(B, S, FOX_HEADS * HEAD_DIM)

        lam_init = 0.8 - 0.6 * math.exp(-0.3 * l)
        lam = (jnp.exp(jnp.sum(diff_lq1[l].astype(jnp.float32) * diff_lk1[l].astype(jnp.float32)))
               - jnp.exp(jnp.sum(diff_lq2[l].astype(jnp.float32) * diff_lk2[l].astype(jnp.float32)))
               + lam_init)
        dq = rms_norm(dq.reshape(B, S, DIFF_HEADS, 2, DIFF_QK_DIM), diff_qn[l]).transpose(0, 2, 3, 1, 4)
        dk = rms_norm(dk.reshape(B, S, DIFF_HEADS, 2, DIFF_QK_DIM), diff_kn[l]).transpose(0, 2, 3, 1, 4)
        dq = partial_rope(dq, cos32, sin32)
        dk = partial_rope(dk, cos32, sin32)
        dv = dv.reshape(B, S, DIFF_HEADS, DIFF_V_DIM).transpose(0, 2, 1, 3)
        o_diff = diff_attention(dq, dk, dv, lam)
        o_diff = (rms_norm(o_diff, diff_subln[l]) * (1.0 - lam_init)).reshape(B, S, DIFF_HEADS * DIFF_V_DIM)

        sq = partial_rope(rms_norm(sq.reshape(B, S, DSA_HEADS, HEAD_DIM), dsa_qn[l]).transpose(0, 2, 1, 3), cos64, sin64)
        sk = partial_rope(rms_norm(sk, dsa_kn[l]), cos64, sin64)
        iq = partial_rope(iq.reshape(B, S, IDX_HEADS, IDX_DIM).transpose(0, 2, 1, 3), cos64, sin64)
        ik = partial_rope(ik, cos64, sin64)
        iw = iw * (IDX_HEADS ** -0.5)
        o_dsa = dsa_attention(sq, sk, sv, iq, ik, iw, topk).reshape(B, S, DSA_HEADS * HEAD_DIM)

        x = x + jnp.concatenate([o_fox, o_diff, o_dsa], axis=-1) @ w_out[l]

        h = rms_norm(x, ffn_norm[l])
        gate, up = jnp.split(h @ w_gate_up[l], 2, axis=-1)
        x = x + (jax.nn.silu(gate) * up) @ w_down[l]
    return x
```

```python
import functools
import math

import numpy as np
import jax
import jax.numpy as jnp
from jax import lax
from jax.experimental import pallas as pl
from jax.experimental.pallas import tpu as pltpu

F32 = jnp.float32
BF16 = jnp.bfloat16
LANES = 128
NORM_EPS = 1e-6
NEG = -1e30
INT_MIN = -(2 ** 31)
ROPE_THETA = 500000.0

HEAD_DIM = 64
FOX_HEADS = 6
DIFF_HEADS = 4
DIFF_QK_DIM = 32
DIFF_V_DIM = 64
DSA_HEADS = 6
IDX_HEADS = 4
DSA_TOPK = 256

C_FQ, C_FK, C_FV = 0, 3, 6
C_DQ, C_DK, C_DV = 9, 11, 13
C_SQ, C_SK, C_SV = 15, 18, 19
C_IQ, C_IK = 20, 22
C_GATES = 23
N_MAIN = 23
N_PROJ = 24

VMEM_LIMIT = 56 * 1024 * 1024


def _dot(a, b):
    return jnp.dot(a, b, preferred_element_type=F32)


def _dot_nt(a, b):
    return lax.dot_general(a, b, (((1,), (1,)), ((), ())), preferred_element_type=F32)


def _block_diag_ones(seg):
    r = lax.broadcasted_iota(jnp.int32, (LANES, LANES), 0) // seg
    c = lax.broadcasted_iota(jnp.int32, (LANES, LANES), 1) // seg
    return jnp.where(r == c, 1.0, 0.0).astype(BF16)


def _seg_sum(v, bd):
    hi = v.astype(BF16)
    lo = (v - hi.astype(F32)).astype(BF16)
    return _dot(hi, bd) + _dot(lo, bd)


def _inproj_kernel(x_ref, gn_ref, w_ref, vec_ref, rope_ref, main_ref, gates_ref, p_ref):
    x = x_ref[...]
    ms = jnp.mean(x * x, axis=-1, keepdims=True)
    h = (x * lax.rsqrt(ms + NORM_EPS) * gn_ref[...]).astype(BF16)
    p_ref[...] = _dot(h, w_ref[...])

    bd64 = _block_diag_ones(64)
    bd32 = _block_diag_ones(32)

    def chunk(c):
        return p_ref[:, c * LANES:(c + 1) * LANES]

    def put(c, y):
        main_ref[:, c * LANES:(c + 1) * LANES] = y.astype(BF16)

    def segnorm(y, bd, seg, row):
        return y * lax.rsqrt(_seg_sum(y * y, bd) * (1.0 / seg) + NORM_EPS) * vec_ref[row:row + 1, :]

    def rope(y, t0, sh):
        return (y * rope_ref[t0] + pltpu.roll(y, sh, 1) * rope_ref[t0 + 1]
                + pltpu.roll(y, LANES - sh, 1) * rope_ref[t0 + 2])

    for c in range(3):
        put(C_FQ + c, segnorm(chunk(C_FQ + c), bd64, 64, 0))
        put(C_FK + c, segnorm(chunk(C_FK + c), bd64, 64, 1))
        put(C_FV + c, chunk(C_FV + c))
        put(C_SQ + c, rope(segnorm(chunk(C_SQ + c), bd64, 64, 4), 0, 8))
    for c in range(2):
        put(C_DQ + c, rope(segnorm(chunk(C_DQ + c), bd32, 32, 2), 3, 4))
        put(C_DK + c, rope(segnorm(chunk(C_DK + c), bd32, 32, 3), 3, 4))
        put(C_DV + c, chunk(C_DV + c))
        put(C_IQ + c, rope(chunk(C_IQ + c), 0, 8))
    put(C_SK, rope(segnorm(chunk(C_SK), bd64, 64, 5), 0, 8))
    put(C_SV, chunk(C_SV))
    put(C_IK, rope(chunk(C_IK), 0, 8))

    z = chunk(C_GATES)
    v = z + vec_ref[6:7, :]
    logsig = jnp.minimum(v, 0.0) - jnp.log1p(jnp.exp(-jnp.abs(v)))
    lane = lax.broadcasted_iota(jnp.int32, z.shape, 1)
    gates_ref[...] = jnp.where(lane < FOX_HEADS, logsig, z * 0.0625)


def _inproj(x2, gn, w_all, vec, ropetab, *, seq, tm):
    rows = x2.shape[0]
    nst = seq // tm
    return pl.pallas_call(
        _inproj_kernel,
        grid=(rows // tm,),
        in_specs=[
            pl.BlockSpec((tm, x2.shape[1]), lambda i: (i, 0)),
            pl.BlockSpec((1, x2.shape[1]), lambda i: (0, 0)),
            pl.BlockSpec(w_all.shape, lambda i: (0, 0), pipeline_mode=pl.Buffered(1)),
            pl.BlockSpec((8, LANES), lambda i: (0, 0)),
            pl.BlockSpec((6, tm, LANES), lambda i: (0, i % nst, 0)),
        ],
        out_specs=[
            pl.BlockSpec((tm, N_MAIN * LANES), lambda i: (i, 0)),
            pl.BlockSpec((tm, LANES), lambda i: (i, 0)),
        ],
        out_shape=[
            jax.ShapeDtypeStruct((rows, N_MAIN * LANES), BF16),
            jax.ShapeDtypeStruct((rows, LANES), F32),
        ],
        scratch_shapes=[pltpu.VMEM((tm, N_PROJ * LANES), F32)],
        compiler_params=pltpu.CompilerParams(vmem_limit_bytes=VMEM_LIMIT),
        name="inproj",
    )(x2, gn, w_all, vec, ropetab)


def _gates_kernel(g_ref, ccol_ref, crow_ref, *, blk):
    n = g_ref.shape[0] // blk
    r = lax.broadcasted_iota(jnp.int32, (blk, blk), 0)
    c = lax.broadcasted_iota(jnp.int32, (blk, blk), 1)
    tri = jnp.where(r >= c, 1.0, 0.0).astype(BF16)
    carry = jnp.zeros((1, LANES), F32)
    for b in range(n):
        gb = g_ref[b * blk:(b + 1) * blk, :]
        hi = gb.astype(BF16)
        r1 = gb - hi.astype(F32)
        mid = r1.astype(BF16)
        lo = (r1 - mid.astype(F32)).astype(BF16)
        cb = _dot(tri, hi) + _dot(tri, mid) + _dot(tri, lo) + carry
        ccol_ref[b * blk:(b + 1) * blk, :] = cb
        crow_ref[b] = cb.T[0:8, :]
        carry = cb[blk - 1:blk, :]


def _gates(gates, *, batch, seq, blk):
    n = seq // blk
    return pl.pallas_call(
        functools.partial(_gates_kernel, blk=blk),
        grid=(batch,),
        in_specs=[pl.BlockSpec((seq, LANES), lambda b: (b, 0))],
        out_specs=[
            pl.BlockSpec((seq, LANES), lambda b: (b, 0)),
            pl.BlockSpec((n, 8, blk), lambda b: (b, 0, 0)),
        ],
        out_shape=[
            jax.ShapeDtypeStruct((batch * seq, LANES), F32),
            jax.ShapeDtypeStruct((batch * n, 8, blk), F32),
        ],
        name="gates",
    )(gates)


def _online_step(s, vb, m_ref, l_ref, acc_ref, idx):
    m_prev = m_ref[idx]
    m_new = jnp.maximum(m_prev, jnp.max(s, axis=-1, keepdims=True))
    alpha = jnp.exp(m_prev - m_new)
    p = jnp.exp(s - m_new)
    l_ref[idx] = alpha * l_ref[idx] + jnp.sum(p, axis=-1, keepdims=True)
    acc_ref[idx] = alpha * acc_ref[idx] + _dot(p.astype(BF16), vb)
    m_ref[idx] = m_new


def _init_state(m_ref, l_ref, acc_ref):
    m_ref[...] = jnp.full(m_ref.shape, NEG, F32)
    l_ref[...] = jnp.zeros(l_ref.shape, F32)
    acc_ref[...] = jnp.zeros(acc_ref.shape, F32)


def _causal(t):
    row = lax.broadcasted_iota(jnp.int32, (t, t), 0)
    col = lax.broadcasted_iota(jnp.int32, (t, t), 1)
    return col <= row


def _fox_kernel(q_ref, k_ref, v_ref, ccol_ref, crow_ref, o_ref, m_ref, l_ref, acc_ref, *, t):
    pair = pl.program_id(1)
    i = pl.program_id(2)
    q = q_ref[...]
    lane = lax.broadcasted_iota(jnp.int32, (t, LANES), 1)
    upper = lane >= HEAD_DIM
    zero = jnp.zeros_like(q)
    qm = (jnp.where(upper, zero, q), jnp.where(upper, q, zero))
    cc = ccol_ref[...]
    ct = tuple(jnp.sum(jnp.where(lane == 2 * pair + hh, cc, 0.0), axis=1, keepdims=True)
               for hh in range(2))
    _init_state(m_ref, l_ref, acc_ref)

    def step(j, diag):
        off = pl.multiple_of(j * t, t)
        kb = k_ref[pl.ds(off, t), :]
        vb = v_ref[pl.ds(off, t), :]
        for hh in range(2):
            cs = crow_ref[j, pl.ds(2 * pair + hh, 1), :]
            s = (_dot_nt(qm[hh], kb) + ct[hh]) - cs
            if diag:
                s = jnp.where(_causal(t), s, NEG)
            _online_step(s, vb, m_ref, l_ref, acc_ref, hh)

    def body(j, carry):
        step(j, False)
        return carry

    lax.fori_loop(0, i, body, 0)
    step(i, True)
    o = jnp.where(upper, acc_ref[1] / l_ref[1], acc_ref[0] / l_ref[0])
    o_ref[...] = o.astype(BF16)


def _fox(main, ccol, crow, *, batch, seq, t):
    nq = seq // t
    npair = FOX_HEADS // 2
    return pl.pallas_call(
        functools.partial(_fox_kernel, t=t),
        grid=(batch, npair, nq),
        in_specs=[
            pl.BlockSpec((t, LANES), lambda b, p, i: (b * nq + i, C_FQ + p)),
            pl.BlockSpec((seq, LANES), lambda b, p, i: (b, C_FK + p)),
            pl.BlockSpec((seq, LANES), lambda b, p, i: (b, C_FV + p)),
            pl.BlockSpec((t, LANES), lambda b, p, i: (b * nq + i, 0)),
            pl.BlockSpec((nq, 8, t), lambda b, p, i: (b, 0, 0)),
        ],
        out_specs=pl.BlockSpec((t, LANES), lambda b, p, i: (b * nq + i, p)),
        out_shape=jax.ShapeDtypeStruct((batch * seq, npair * LANES), BF16),
        scratch_shapes=[
            pltpu.VMEM((2, t, 1), F32),
            pltpu.VMEM((2, t, 1), F32),
            pltpu.VMEM((2, t, LANES), F32),
        ],
        name="fox",
    )(main, main, main, ccol, crow)


def _diff_kernel(lam_ref, q_ref, k_ref, v_ref, g_ref, o_ref, m_ref, l_ref, acc_ref, *, t):
    i = pl.program_id(2)
    q = q_ref[...]
    lane = lax.broadcasted_iota(jnp.int32, (t, LANES), 1)
    zero = jnp.zeros_like(q)
    qm = tuple(jnp.where((lane // DIFF_QK_DIM) == n, q, zero) for n in range(4))
    _init_state(m_ref, l_ref, acc_ref)

    def step(j, diag):
        off = pl.multiple_of(j * t, t)
        kb = k_ref[pl.ds(off, t), :]
        vb = v_ref[pl.ds(off, t), :]
        for n in range(4):
            s = _dot_nt(qm[n], kb)
            if diag:
                s = jnp.where(_causal(t), s, NEG)
            _online_step(s, vb, m_ref, l_ref, acc_ref, n)

    def body(j, carry):
        step(j, False)
        return carry

    lax.fori_loop(0, i, body, 0)
    step(i, True)
    lam = lam_ref[0, 0]
    o0 = acc_ref[0] / l_ref[0] - lam * (acc_ref[1] / l_ref[1])
    o1 = acc_ref[2] / l_ref[2] - lam * (acc_ref[3] / l_ref[3])
    o = jnp.where(lane >= DIFF_V_DIM, o1, o0)
    ss = _seg_sum(o * o, _block_diag_ones(DIFF_V_DIM))
    o = o * lax.rsqrt(ss * (1.0 / DIFF_V_DIM) + NORM_EPS) * g_ref[...]
    o_ref[...] = o.astype(BF16)


def _diff(lam, main, gsub, *, batch, seq, t):
    nq = seq // t
    npair = DIFF_HEADS // 2
    return pl.pallas_call(
        functools.partial(_diff_kernel, t=t),
        grid=(batch, npair, nq),
        in_specs=[
            pl.BlockSpec(memory_space=pltpu.SMEM),
            pl.BlockSpec((t, LANES), lambda b, p, i: (b * nq + i, C_DQ + p)),
            pl.BlockSpec((seq, LANES), lambda b, p, i: (b, C_DK + p)),
            pl.BlockSpec((seq, LANES), lambda b, p, i: (b, C_DV + p)),
            pl.BlockSpec((1, LANES), lambda b, p, i: (0, 0)),
        ],
        out_specs=pl.BlockSpec((t, LANES), lambda b, p, i: (b * nq + i, p)),
        out_shape=jax.ShapeDtypeStruct((batch * seq, npair * LANES), BF16),
        scratch_shapes=[
            pltpu.VMEM((4, t, 1), F32),
            pltpu.VMEM((4, t, 1), F32),
            pltpu.VMEM((4, t, LANES), F32),
        ],
        name="diff",
    )(lam, main, main, main, gsub)


def _dsa_kernel(iq_ref, ik_ref, g_ref, sq_ref, sk_ref, sv_ref, o_ref,
                key_ref, m_ref, l_ref, acc_ref, thr_ref, *, t, topk):
    i = pl.program_id(1)
    nkv = i + 1
    lane = lax.broadcasted_iota(jnp.int32, (t, LANES), 1)
    upper = lane >= HEAD_DIM
    causal = _causal(t)

    iq = iq_ref[...]
    zero = jnp.zeros((t, LANES), BF16)
    iqm = []
    for h in range(IDX_HEADS):
        qc = iq[:, (h // 2) * LANES:(h // 2 + 1) * LANES]
        iqm.append(jnp.where(upper, qc, zero) if h % 2 else jnp.where(upper, zero, qc))
    g = g_ref[...]
    wts = [g[:, FOX_HEADS + h:FOX_HEADS + h + 1] for h in range(IDX_HEADS)]

    def score_chunk(j, diag):
        off = pl.multiple_of(j * t, t)
        kb = ik_ref[pl.ds(off, t), :]
        sc = wts[0] * jnp.maximum(_dot_nt(iqm[0], kb), 0.0)
        for h in range(1, IDX_HEADS):
            sc = sc + wts[h] * jnp.maximum(_dot_nt(iqm[h], kb), 0.0)
        bits = lax.bitcast_convert_type(sc, jnp.int32)
        key = jnp.where(bits < 0, bits ^ 0x7FFFFFFF, bits)
        key = jnp.where(key == -1, 0, key)
        if diag:
            key = jnp.where(causal, key, INT_MIN)
        key_ref[j] = key

    def score_body(j, carry):
        score_chunk(j, False)
        return carry

    lax.fori_loop(0, i, score_body, 0)
    score_chunk(i, True)

    def count(pred_fn):
        def cbody(j, acc):
            return acc + jnp.where(pred_fn(key_ref[j]), 1, 0)
        acc = lax.fori_loop(0, nkv, cbody, jnp.zeros((t, t), jnp.int32))
        return jnp.sum(acc, axis=1, keepdims=True)

    thr_ref[...] = jnp.full((t, 1), INT_MIN, jnp.int32)

    @pl.when(nkv * t > topk)
    def _():
        def bit_body(it, thr):
            cand = thr + lax.shift_left(jnp.int32(1), 31 - it)
            cnt = count(lambda k: k >= cand)
            return jnp.where(cnt >= topk, cand, thr)
        thr_ref[...] = lax.fori_loop(0, 32, bit_body, jnp.full((t, 1), INT_MIN, jnp.int32))

    thr = thr_ref[...]
    n_gt = count(lambda k: k > thr)
    ties_kept = (topk - n_gt).astype(F32)

    sq = sq_ref[...]
    sqm = []
    for h in range(DSA_HEADS):
        qc = sq[:, (h // 2) * LANES:(h // 2 + 1) * LANES]
        sqm.append(jnp.where(upper, qc, zero) if h % 2 else jnp.where(upper, zero, qc))
    _init_state(m_ref, l_ref, acc_ref)
    r = lax.broadcasted_iota(jnp.int32, (t, t), 0)
    c = lax.broadcasted_iota(jnp.int32, (t, t), 1)
    before = jnp.where(r < c, 1.0, 0.0).astype(BF16)

    def att_chunk(j, diag, seen):
        off = pl.multiple_of(j * t, t)
        key = key_ref[j]
        eq = key == thr
        eqf = jnp.where(eq, 1.0, 0.0)
        rank = _dot(eqf.astype(BF16), before) + seen
        sel = (key > thr) | (eq & (rank < ties_kept))
        if diag:
            sel = sel & causal
        bias = jnp.where(sel, 0.0, NEG)
        kb = sk_ref[pl.ds(off, t), :]
        vb = sv_ref[pl.ds(off, t), :]
        for h in range(DSA_HEADS):
            _online_step(_dot_nt(sqm[h], kb) + bias, vb, m_ref, l_ref, acc_ref, h)
        return seen + jnp.sum(eqf, axis=1, keepdims=True)

    seen = lax.fori_loop(0, i, lambda j, s: att_chunk(j, False, s), jnp.zeros((t, 1), F32))
    att_chunk(i, True, seen)
    for cpair in range(DSA_HEADS // 2):
        o = jnp.where(upper, acc_ref[2 * cpair + 1] / l_ref[2 * cpair + 1],
                      acc_ref[2 * cpair] / l_ref[2 * cpair])
        o_ref[:, cpair * LANES:(cpair + 1) * LANES] = o.astype(BF16)


def _dsa(main, gates, *, batch, seq, t, topk):
    nq = seq // t
    nsq = DSA_HEADS // 2
    niq = IDX_HEADS // 2
    return pl.pallas_call(
        functools.partial(_dsa_kernel, t=t, topk=topk),
        grid=(batch, nq),
        in_specs=[
            pl.BlockSpec((t, niq * LANES), lambda b, i: (b * nq + i, C_IQ // niq)),
            pl.BlockSpec((seq, LANES), lambda b, i: (b, C_IK)),
            pl.BlockSpec((t, LANES), lambda b, i: (b * nq + i, 0)),
            pl.BlockSpec((t, nsq * LANES), lambda b, i: (b * nq + i, C_SQ // nsq)),
            pl.BlockSpec((seq, LANES), lambda b, i: (b, C_SK)),
            pl.BlockSpec((seq, LANES), lambda b, i: (b, C_SV)),
        ],
        out_specs=pl.BlockSpec((t, nsq * LANES), lambda b, i: (b * nq + i, 0)),
        out_shape=jax.ShapeDtypeStruct((batch * seq, nsq * LANES), BF16),
        scratch_shapes=[
            pltpu.VMEM((nq, t, t), jnp.int32),
            pltpu.VMEM((DSA_HEADS, t, 1), F32),
            pltpu.VMEM((DSA_HEADS, t, 1), F32),
            pltpu.VMEM((DSA_HEADS, t, LANES), F32),
            pltpu.VMEM((t, 1), jnp.int32),
        ],
        name="dsa",
    )(main, main, gates, main, main, main)


def _ffn_kernel(x_ref, of_ref, od_ref, os_ref, wof_ref, wod_ref, wos_ref, gn_ref,
                wg_ref, wu_ref, wd_ref, o_ref, acc_ref):
    x1 = (x_ref[...] + _dot(of_ref[...], wof_ref[...]) + _dot(od_ref[...], wod_ref[...])
          + _dot(os_ref[...], wos_ref[...]))
    ms = jnp.mean(x1 * x1, axis=-1, keepdims=True)
    h = (x1 * lax.rsqrt(ms + NORM_EPS) * gn_ref[...]).astype(BF16)
    acc_ref[...] = x1

    def body(c, carry):
        gate = _dot(h, wg_ref[c])
        up = _dot(h, wu_ref[c])
        a = (gate * jax.nn.sigmoid(gate) * up).astype(BF16)
        acc_ref[...] += _dot(a, wd_ref[c])
        return carry

    lax.fori_loop(0, wg_ref.shape[0], body, 0)
    o_ref[...] = acc_ref[...]


def _ffn(x2, o_fox, o_diff, o_dsa, wof, wod, wos, gn, wg, wu, wd, *, tm):
    rows, d = x2.shape

    def const(a):
        nd = a.ndim
        return pl.BlockSpec(a.shape, lambda i: (0,) * nd, pipeline_mode=pl.Buffered(1))

    def rowblk(a):
        return pl.BlockSpec((tm, a.shape[1]), lambda i: (i, 0))

    return pl.pallas_call(
        _ffn_kernel,
        grid=(rows // tm,),
        in_specs=[rowblk(x2), rowblk(o_fox), rowblk(o_diff), rowblk(o_dsa),
                  const(wof), const(wod), const(wos), const(gn), const(wg), const(wu), const(wd)],
        out_specs=pl.BlockSpec((tm, d), lambda i: (i, 0)),
        out_shape=jax.ShapeDtypeStruct((rows, d), F32),
        scratch_shapes=[pltpu.VMEM((tm, d), F32)],
        compiler_params=pltpu.CompilerParams(vmem_limit_bytes=VMEM_LIMIT),
        name="ffn",
    )(x2, o_fox, o_diff, o_dsa, wof, wod, wos, gn, wg, wu, wd)


def _relayout_w_in(w):
    sizes = [FOX_HEADS * HEAD_DIM] * 3 + [FOX_HEADS] + [DIFF_HEADS * 2 * DIFF_QK_DIM] * 2 + \
            [DIFF_HEADS * DIFF_V_DIM, DSA_HEADS * HEAD_DIM, HEAD_DIM, HEAD_DIM,
             IDX_HEADS * HEAD_DIM, HEAD_DIM, IDX_HEADS]
    starts = np.concatenate([[0], np.cumsum(sizes)])
    fq, fk, fv, ff, dq, dk, dv, sq, sk, sv, iq, ik, iw = [w[:, int(starts[n]):int(starts[n + 1])]
                                                          for n in range(len(sizes))]
    pad = jnp.zeros((w.shape[0], LANES - FOX_HEADS - IDX_HEADS), w.dtype)
    w_all = jnp.concatenate([fq, fk, fv, dq, dk, dv, sq, sk, sk, sv, sv, iq, ik, ik, ff, iw, pad],
                            axis=1)
    assert w_all.shape[1] == N_PROJ * LANES
    return w_all.astype(BF16)


def _rope_tables(seq, head_dim):
    rot = head_dim // 4
    half = rot // 2
    inv_freq = 1.0 / (ROPE_THETA ** (jnp.arange(0, rot, 2, dtype=F32) / rot))
    ang = jnp.arange(seq, dtype=F32)[:, None] * inv_freq[None, :]
    cos, sin = jnp.cos(ang), jnp.sin(ang)
    r = np.arange(LANES) % head_dim
    f = r % half
    cos_t = jnp.where(r < rot, cos[:, f], 1.0)
    sin_hi = jnp.where((r >= half) & (r < rot), sin[:, f], 0.0)
    sin_lo = jnp.where(r < half, -sin[:, f], 0.0)
    return [cos_t, sin_hi, sin_lo]


def _pad_lanes(v):
    return jnp.pad(v.astype(F32), (0, LANES - v.shape[0]))


@jax.jit
def kernel(x, attn_norm, w_in, fox_fb, fox_qn, fox_kn, diff_qn, diff_kn, diff_lq1, diff_lk1,
           diff_lq2, diff_lk2, diff_subln, dsa_qn, dsa_kn, w_out, ffn_norm, w_gate_up, w_down):
    batch, seq, d = x.shape
    depth = w_in.shape[0]
    hidden = w_down.shape[1]
    topk = min(DSA_TOPK, seq // 4)
    t_att = 256 if seq % 256 == 0 else 128
    t_dsa = 128
    tm = 512 if (batch * seq) % 512 == 0 else 256
    tm_in = min(tm, seq)
    ffn_chunk = 256
    assert hidden % ffn_chunk == 0 and seq % t_att == 0 and seq % tm_in == 0

    ropetab = jnp.stack(_rope_tables(seq, HEAD_DIM) + _rope_tables(seq, DIFF_QK_DIM))
    n_off = FOX_HEADS * HEAD_DIM
    n_od = DIFF_HEADS * DIFF_V_DIM

    x2 = x.reshape(batch * seq, d)
    for l in range(depth):
        w_all = _relayout_w_in(w_in[l])
        vec = jnp.stack([
            jnp.tile(fox_qn[l].astype(F32), 2) * HEAD_DIM ** -0.5,
            jnp.tile(fox_kn[l].astype(F32), 2),
            jnp.tile(diff_qn[l].astype(F32), 4) * DIFF_QK_DIM ** -0.5,
            jnp.tile(diff_kn[l].astype(F32), 4),
            jnp.tile(dsa_qn[l].astype(F32), 2) * HEAD_DIM ** -0.5,
            jnp.tile(dsa_kn[l].astype(F32), 2),
            _pad_lanes(fox_fb[l]),
            jnp.zeros((LANES,), F32),
        ])
        main, gates = _inproj(x2, attn_norm[l].astype(F32)[None, :], w_all, vec, ropetab,
                              seq=seq, tm=tm_in)
        ccol, crow = _gates(gates, batch=batch, seq=seq, blk=t_att)
        o_fox = _fox(main, ccol, crow, batch=batch, seq=seq, t=t_att)

        lam_init = 0.8 - 0.6 * math.exp(-0.3 * l)
        lam = (jnp.exp(jnp.sum(diff_lq1[l].astype(F32) * diff_lk1[l].astype(F32)))
               - jnp.exp(jnp.sum(diff_lq2[l].astype(F32) * diff_lk2[l].astype(F32))) + lam_init)
        gsub = (jnp.tile(diff_subln[l].astype(F32), 2) * (1.0 - lam_init))[None, :]
        o_diff = _diff(lam.reshape(1, 1), main, gsub, batch=batch, seq=seq, t=t_att)

        o_dsa = _dsa(main, gates, batch=batch, seq=seq, t=t_dsa, topk=topk)

        wo = w_out[l].astype(BF16)
        wgu = w_gate_up[l].astype(BF16)
        nch = hidden // ffn_chunk
        wg = wgu[:, :hidden].reshape(d, nch, ffn_chunk).transpose(1, 0, 2)
        wu = wgu[:, hidden:].reshape(d, nch, ffn_chunk).transpose(1, 0, 2)
        wd = w_down[l].astype(BF16).reshape(nch, ffn_chunk, d)
        x2 = _ffn(x2, o_fox, o_diff, o_dsa, wo[:n_off], wo[n_off:n_off + n_od], wo[n_off + n_od:],
                  ffn_norm[l].astype(F32)[None, :], wg, wu, wd, tm=tm)
    return x2.reshape(batch, seq, d)
```

```python
import functools
import math

import numpy as np
import jax
import jax.numpy as jnp
from jax import lax
from jax.experimental import pallas as pl
from jax.experimental.pallas import tpu as pltpu

F32 = jnp.float32
BF16 = jnp.bfloat16
LANES = 128
SUBLANES = 8
NORM_EPS = 1e-6
NEG = -1e30
INT_MIN = -(2 ** 31)
ROPE_THETA = 500000.0
LOG2E = 1.4426950408889634

HEAD_DIM = 64
FOX_HEADS = 6
DIFF_HEADS = 4
DIFF_QK_DIM = 32
DIFF_V_DIM = 64
DSA_HEADS = 6
IDX_HEADS = 4
DSA_TOPK = 256

P_FQ, P_FK, P_FV = 0, 3, 6
P_DQ, P_DK, P_DV = 9, 11, 13
P_SQ, P_SK, P_SV = 15, 18, 19
P_IQ, P_IK, P_GATES = 20, 22, 23
N_PROJ = 24
C_FQ, C_FK, C_SQ, C_SK = 0, 3, 6, 9
C_IQ, C_DQ, C_DK = 10, 12, 14
C_SVA, C_IK = 16, 17
C_FVA, C_DVA = 18, 24
N_MAIN = 28
CS_LANE0, CT_LANE0 = 0, 32

VMEM_LIMIT = 56 * 1024 * 1024


def _dot(a, b):
    return jnp.dot(a, b, preferred_element_type=F32)


def _dot_nt(a, b):
    return lax.dot_general(a, b, (((1,), (1,)), ((), ())), preferred_element_type=F32)


def _block_diag_ones(seg):
    r = lax.broadcasted_iota(jnp.int32, (LANES, LANES), 0) // seg
    c = lax.broadcasted_iota(jnp.int32, (LANES, LANES), 1) // seg
    return jnp.where(r == c, 1.0, 0.0).astype(BF16)


def _split3(v):
    hi = v.astype(BF16)
    r1 = v - hi.astype(F32)
    mid = r1.astype(BF16)
    lo = (r1 - mid.astype(F32)).astype(BF16)
    return hi, mid, lo


def _seg_sum(v, bd):
    hi = v.astype(BF16)
    lo = (v - hi.astype(F32)).astype(BF16)
    return _dot(hi, bd) + _dot(lo, bd)


def _inproj_kernel(x_ref, gn_ref, w_ref, vec_ref, rope_ref, main_ref, gates_ref, p_ref):
    x = x_ref[...]
    ms = jnp.mean(x * x, axis=-1, keepdims=True)
    h = (x * lax.rsqrt(ms + NORM_EPS) * gn_ref[...]).astype(BF16)
    p_ref[...] = _dot(h, w_ref[...])

    bd64 = _block_diag_ones(64)
    bd32 = _block_diag_ones(32)
    lane = lax.broadcasted_iota(jnp.int32, (x.shape[0], LANES), 1)
    upper = lane >= HEAD_DIM

    def chunk(c):
        return p_ref[:, c * LANES:(c + 1) * LANES]

    def put(c, y):
        main_ref[:, c * LANES:(c + 1) * LANES] = y.astype(BF16)

    def put_v_ones(c, y):
        put(c, jnp.where(upper, 1.0, y))
        put(c + 1, jnp.where(upper, 1.0, pltpu.roll(y, HEAD_DIM, 1)))

    def segnorm(y, bd, seg, row):
        return y * lax.rsqrt(_seg_sum(y * y, bd) * (1.0 / seg) + NORM_EPS) * vec_ref[row:row + 1, :]

    def rope(y, t0, sh):
        return (y * rope_ref[t0] + pltpu.roll(y, sh, 1) * rope_ref[t0 + 1]
                + pltpu.roll(y, LANES - sh, 1) * rope_ref[t0 + 2])

    for c in range(3):
        put(C_FQ + c, segnorm(chunk(P_FQ + c), bd64, 64, 0))
        put(C_FK + c, segnorm(chunk(P_FK + c), bd64, 64, 1))
        put_v_ones(C_FVA + 2 * c, chunk(P_FV + c))
        put(C_SQ + c, rope(segnorm(chunk(P_SQ + c), bd64, 64, 4), 0, 8))
    for c in range(2):
        put(C_DQ + c, rope(segnorm(chunk(P_DQ + c), bd32, 32, 2), 3, 4))
        put(C_DK + c, rope(segnorm(chunk(P_DK + c), bd32, 32, 3), 3, 4))
        put_v_ones(C_DVA + 2 * c, chunk(P_DV + c))
        put(C_IQ + c, rope(chunk(P_IQ + c), 0, 8))
    put(C_SK, rope(segnorm(chunk(P_SK), bd64, 64, 5), 0, 8))
    put(C_SVA, jnp.where(upper, 1.0, chunk(P_SV)))
    put(C_IK, rope(chunk(P_IK), 0, 8))

    z = chunk(P_GATES)
    v = z + vec_ref[6:7, :]
    logsig = jnp.minimum(v, 0.0) - jnp.log1p(jnp.exp(-jnp.abs(v)))
    gates_ref[...] = jnp.where(lane < FOX_HEADS, logsig, z * 0.0625)


def _inproj(x2, gn, w_all, vec, ropetab, *, seq, tm):
    rows = x2.shape[0]
    nst = seq // tm
    return pl.pallas_call(
        _inproj_kernel,
        grid=(rows // tm,),
        in_specs=[
            pl.BlockSpec((tm, x2.shape[1]), lambda i: (i, 0)),
            pl.BlockSpec((1, x2.shape[1]), lambda i: (0, 0)),
            pl.BlockSpec(w_all.shape, lambda i: (0, 0), pipeline_mode=pl.Buffered(1)),
            pl.BlockSpec((8, LANES), lambda i: (0, 0)),
            pl.BlockSpec((6, tm, LANES), lambda i: (0, i % nst, 0)),
        ],
        out_specs=[
            pl.BlockSpec((tm, N_MAIN * LANES), lambda i: (i, 0)),
            pl.BlockSpec((tm, LANES), lambda i: (i, 0)),
        ],
        out_shape=[
            jax.ShapeDtypeStruct((rows, N_MAIN * LANES), BF16),
            jax.ShapeDtypeStruct((rows, LANES), F32),
        ],
        scratch_shapes=[pltpu.VMEM((tm, N_PROJ * LANES), F32)],
        compiler_params=pltpu.CompilerParams(vmem_limit_bytes=VMEM_LIMIT),
        name="inproj",
    )(x2, gn, w_all, vec, ropetab)


def _gates_kernel(g_ref, ks_ref, qs_ref, *, blk):
    n = g_ref.shape[0] // blk
    r = lax.broadcasted_iota(jnp.int32, (blk, blk), 0)
    c = lax.broadcasted_iota(jnp.int32, (blk, blk), 1)
    tri = jnp.where(r >= c, 1.0, 0.0).astype(BF16)
    er = lax.broadcasted_iota(jnp.int32, (LANES, LANES), 0)
    ec = lax.broadcasted_iota(jnp.int32, (LANES, LANES), 1)
    head = er < FOX_HEADS

    def place(lane0, part, val):
        return jnp.where(head & (ec == lane0 + 3 * er + part), val, 0.0).astype(BF16)

    lane = lax.broadcasted_iota(jnp.int32, (blk, LANES), 1)
    k_ones = jnp.where((lane >= CT_LANE0) & (lane < CT_LANE0 + 3 * FOX_HEADS), 1.0, 0.0)
    carry = jnp.zeros((1, LANES), F32)
    for b in range(n):
        parts = _split3(g_ref[b * blk:(b + 1) * blk, :])
        cb = _dot(tri, parts[0]) + _dot(tri, parts[1]) + _dot(tri, parts[2]) + carry
        carry = cb[blk - 1:blk, :]
        c2 = _split3(cb * LOG2E)
        ks = k_ones
        qs = jnp.zeros((blk, LANES), F32)
        for part in range(3):
            ks = ks + _dot(c2[part], place(CS_LANE0, part, -1.0))
            qs = qs + _dot(c2[part], place(CT_LANE0, part, 1.0))
        ks_ref[b * blk:(b + 1) * blk, :] = ks.astype(BF16)
        qs_ref[b * blk:(b + 1) * blk, :] = qs.astype(BF16)


def _gates(gates, *, batch, seq, blk):
    return pl.pallas_call(
        functools.partial(_gates_kernel, blk=blk),
        grid=(batch,),
        in_specs=[pl.BlockSpec((seq, LANES), lambda b: (b, 0))],
        out_specs=[pl.BlockSpec((seq, LANES), lambda b: (b, 0)),
                   pl.BlockSpec((seq, LANES), lambda b: (b, 0))],
        out_shape=[jax.ShapeDtypeStruct((batch * seq, LANES), BF16),
                   jax.ShapeDtypeStruct((batch * seq, LANES), BF16)],
        name="gates",
    )(gates)


def _causal_rows(n, t):
    assert t & (t - 1) == 0
    row = lax.broadcasted_iota(jnp.int32, (n * t, t), 0) & (t - 1)
    col = lax.broadcasted_iota(jnp.int32, (n * t, t), 1)
    return col <= row


def _store_logits(s, j, s_ref, mx_ref):
    s_ref[j] = s
    m = mx_ref[...]
    for c in range(s.shape[1] // LANES):
        m = jnp.maximum(m, s[:, c * LANES:(c + 1) * LANES])
    mx_ref[...] = m


def _for_blocks(n, fn):
    def body(jj, carry):
        fn(2 * jj)
        fn(2 * jj + 1)
        return carry

    lax.fori_loop(0, n // 2, body, 0)

    @pl.when(n % 2 == 1)
    def _():
        fn(n - 1)


def _weighted_values(n, t, s_ref, mx_ref, acc_ref, v_ref, groups):
    m = jnp.max(mx_ref[...], axis=1, keepdims=True)
    acc_ref[...] = jnp.zeros(acc_ref.shape, F32)
    rows = acc_ref.shape[0] // groups

    def pv(j):
        off = pl.multiple_of(j * t, t)
        p = jnp.exp2(s_ref[j] - m).astype(BF16)
        return [_dot(p[g * rows:(g + 1) * rows, :], v_ref[pl.ds(off, t), g * LANES:(g + 1) * LANES])
                for g in range(groups)]

    def add(parts):
        for g in range(groups):
            acc_ref[g * rows:(g + 1) * rows, :] += parts[g]

    def body(jj, carry):
        add([a + b for a, b in zip(pv(2 * jj), pv(2 * jj + 1))])
        return carry

    lax.fori_loop(0, n // 2, body, 0)

    @pl.when(n % 2 == 1)
    def _():
        add(pv(n - 1))


def _normalised(acc):
    return acc / pltpu.roll(acc, HEAD_DIM, 1)


def _fox_kernel(q_ref, qs_ref, k_ref, ks_ref, v_ref, o_ref, s_ref, mx_ref, acc_ref, *, t):
    i = pl.program_id(1)
    npair = FOX_HEADS // 2
    qs = qs_ref[...].astype(F32)
    lane = lax.broadcasted_iota(jnp.int32, (t, LANES), 1)
    upper = lane >= HEAD_DIM
    zero = jnp.zeros((t, LANES), BF16)

    def side(h):
        cs0 = CS_LANE0 + 3 * h
        ct0 = CT_LANE0 + 3 * h
        pick = jnp.where((lane >= cs0) & (lane < cs0 + 3), 1.0, 0.0)
        return jnp.where((lane >= ct0) & (lane < ct0 + 3), qs, pick).astype(BF16)

    qa = []
    for p in range(npair):
        q = q_ref[:, p * LANES:(p + 1) * LANES]
        qa.append(jnp.concatenate([
            jnp.concatenate([jnp.where(upper, zero, q), side(2 * p)], axis=1),
            jnp.concatenate([jnp.where(upper, q, zero), side(2 * p + 1)], axis=1)], axis=0))
    mx_ref[...] = jnp.full(mx_ref.shape, NEG, F32)

    def logits(j):
        off = pl.multiple_of(j * t, t)
        ksb = ks_ref[pl.ds(off, t), :]
        return jnp.concatenate([
            _dot_nt(qa[p], jnp.concatenate([k_ref[pl.ds(off, t), p * LANES:(p + 1) * LANES], ksb],
                                           axis=1))
            for p in range(npair)], axis=0)

    _for_blocks(i, lambda j: _store_logits(logits(j), j, s_ref, mx_ref))
    _store_logits(jnp.where(_causal_rows(FOX_HEADS, t), logits(i), NEG), i, s_ref, mx_ref)
    _weighted_values(i + 1, t, s_ref, mx_ref, acc_ref, v_ref, FOX_HEADS)
    o = _normalised(acc_ref[...])
    for p in range(npair):
        lo = o[2 * p * t:(2 * p + 1) * t]
        hi = pltpu.roll(o[(2 * p + 1) * t:(2 * p + 2) * t], HEAD_DIM, 1)
        o_ref[:, p * LANES:(p + 1) * LANES] = jnp.where(upper, hi, lo).astype(BF16)


def _fox(main, ks, qs, *, batch, seq, t):
    nq = seq // t
    npair = FOX_HEADS // 2
    return pl.pallas_call(
        functools.partial(_fox_kernel, t=t),
        grid=(batch, nq),
        in_specs=[
            pl.BlockSpec((t, npair * LANES), lambda b, i: (b * nq + i, C_FQ // npair)),
            pl.BlockSpec((t, LANES), lambda b, i: (b * nq + i, 0)),
            pl.BlockSpec((seq, npair * LANES), lambda b, i: (b, C_FK // npair)),
            pl.BlockSpec((seq, LANES), lambda b, i: (b, 0)),
            pl.BlockSpec((seq, FOX_HEADS * LANES), lambda b, i: (b, C_FVA // FOX_HEADS)),
        ],
        out_specs=pl.BlockSpec((t, npair * LANES), lambda b, i: (b * nq + i, 0)),
        out_shape=jax.ShapeDtypeStruct((batch * seq, npair * LANES), BF16),
        scratch_shapes=[
            pltpu.VMEM((nq, FOX_HEADS * t, t), F32),
            pltpu.VMEM((FOX_HEADS * t, LANES), F32),
            pltpu.VMEM((FOX_HEADS * t, LANES), F32),
        ],
        compiler_params=pltpu.CompilerParams(vmem_limit_bytes=VMEM_LIMIT),
        name="fox",
    )(main, qs, main, ks, main)


def _diff_kernel(lam_ref, q_ref, k_ref, v_ref, g_ref, o_ref, s_ref, mx_ref, acc_ref, *, t):
    i = pl.program_id(1)
    npair = DIFF_HEADS // 2
    nmaps = 2 * DIFF_HEADS
    lane = lax.broadcasted_iota(jnp.int32, (t, LANES), 1)
    upper = lane >= DIFF_V_DIM
    zero = jnp.zeros((t, LANES), BF16)
    qa = []
    for p in range(npair):
        q = q_ref[:, p * LANES:(p + 1) * LANES]
        qa.append(jnp.concatenate([jnp.where((lane // DIFF_QK_DIM) == n, q, zero) for n in range(4)],
                                  axis=0))
    mx_ref[...] = jnp.full(mx_ref.shape, NEG, F32)

    def logits(j):
        off = pl.multiple_of(j * t, t)
        return jnp.concatenate([_dot_nt(qa[p], k_ref[pl.ds(off, t), p * LANES:(p + 1) * LANES])
                                for p in range(npair)], axis=0)

    _for_blocks(i, lambda j: _store_logits(logits(j), j, s_ref, mx_ref))
    _store_logits(jnp.where(_causal_rows(nmaps, t), logits(i), NEG), i, s_ref, mx_ref)
    _weighted_values(i + 1, t, s_ref, mx_ref, acc_ref, v_ref, DIFF_HEADS)
    o = _normalised(acc_ref[...])
    lam = lam_ref[0, 0]
    bd = _block_diag_ones(DIFF_V_DIM)
    for p in range(npair):
        heads = []
        for h in (2 * p, 2 * p + 1):
            heads.append(o[2 * h * t:(2 * h + 1) * t] - lam * o[(2 * h + 1) * t:(2 * h + 2) * t])
        y = jnp.where(upper, pltpu.roll(heads[1], DIFF_V_DIM, 1), heads[0])
        ss = _seg_sum(y * y, bd)
        y = y * lax.rsqrt(ss * (1.0 / DIFF_V_DIM) + NORM_EPS) * g_ref[...]
        o_ref[:, p * LANES:(p + 1) * LANES] = y.astype(BF16)


def _diff(lam, main, gsub, *, batch, seq, t):
    nq = seq // t
    npair = DIFF_HEADS // 2
    nmaps = 2 * DIFF_HEADS
    return pl.pallas_call(
        functools.partial(_diff_kernel, t=t),
        grid=(batch, nq),
        in_specs=[
            pl.BlockSpec(memory_space=pltpu.SMEM),
            pl.BlockSpec((t, npair * LANES), lambda b, i: (b * nq + i, C_DQ // npair)),
            pl.BlockSpec((seq, npair * LANES), lambda b, i: (b, C_DK // npair)),
            pl.BlockSpec((seq, DIFF_HEADS * LANES), lambda b, i: (b, C_DVA // DIFF_HEADS)),
            pl.BlockSpec((1, LANES), lambda b, i: (0, 0)),
        ],
        out_specs=pl.BlockSpec((t, npair * LANES), lambda b, i: (b * nq + i, 0)),
        out_shape=jax.ShapeDtypeStruct((batch * seq, npair * LANES), BF16),
        scratch_shapes=[
            pltpu.VMEM((nq, nmaps * t, t), F32),
            pltpu.VMEM((nmaps * t, LANES), F32),
            pltpu.VMEM((nmaps * t, LANES), F32),
        ],
        compiler_params=pltpu.CompilerParams(vmem_limit_bytes=VMEM_LIMIT),
        name="diff",
    )(lam, main, main, main, gsub)


def _dsa_kernel(iq_ref, ik_ref, g_ref, sq_ref, sk_ref, sv_ref, o_ref,
                key_ref, s_ref, mx_ref, acc_ref, seen_ref, *, t, topk):
    i = pl.program_id(1)
    lane = lax.broadcasted_iota(jnp.int32, (t, LANES), 1)
    upper = lane >= HEAD_DIM
    zero = jnp.zeros((t, LANES), BF16)
    kpos = lax.broadcasted_iota(jnp.int32, (t, t), 0)
    qpos = lax.broadcasted_iota(jnp.int32, (t, t), 1)
    causal_t = kpos <= qpos

    def head_rows(x, nheads):
        out = []
        for h in range(nheads):
            xc = x[:, (h // 2) * LANES:(h // 2 + 1) * LANES]
            out.append(jnp.where(upper, xc, zero) if h % 2 else jnp.where(upper, zero, xc))
        return out

    iqm = head_rows(iq_ref[...], IDX_HEADS)
    g_t = g_ref[...].T
    wts = [g_t[FOX_HEADS + h:FOX_HEADS + h + 1, :] for h in range(IDX_HEADS)]

    def score_block(j, diag):
        off = pl.multiple_of(j * t, t)
        kb = ik_ref[pl.ds(off, t), :]
        sc = wts[0] * jnp.maximum(_dot_nt(kb, iqm[0]), 0.0)
        for h in range(1, IDX_HEADS):
            sc = sc + wts[h] * jnp.maximum(_dot_nt(kb, iqm[h]), 0.0)
        bits = lax.bitcast_convert_type(sc, jnp.int32)
        key = jnp.where(bits < 0, jnp.int32(INT_MIN) - bits, bits)
        if diag:
            key = jnp.where(causal_t, key, INT_MIN)
        key_ref[j] = key

    _for_blocks(i, lambda j: score_block(j, False))
    score_block(i, True)

    def count(pred_fn):
        def cbody(j, acc):
            hit = jnp.where(pred_fn(key_ref[j]), 1, 0)
            return acc + jnp.sum(hit.reshape(t // SUBLANES, SUBLANES, t), axis=0)
        acc = lax.fori_loop(0, i + 1, cbody, jnp.zeros((SUBLANES, t), jnp.int32))
        return jnp.sum(acc, axis=0, keepdims=True)

    def bit_body(it, thr):
        cand = thr + lax.shift_left(jnp.int32(1), 31 - it)
        return jnp.where(count(lambda k: k >= cand) >= topk, cand, thr)

    thr = lax.fori_loop(0, 32, bit_body, jnp.full((1, t), INT_MIN, jnp.int32))
    ties_kept = (topk - count(lambda k: k > thr)).astype(F32)

    qa = jnp.concatenate(head_rows(sq_ref[...], DSA_HEADS), axis=0)
    mx_ref[...] = jnp.full(mx_ref.shape, NEG, F32)
    earlier = jnp.where(qpos < kpos, 1.0, 0.0).astype(BF16)

    seen_ref[...] = jnp.zeros(seen_ref.shape, F32)

    def logits_block(j, diag):
        off = pl.multiple_of(j * t, t)
        key = key_ref[j]
        eq = key == thr
        eqf = jnp.where(eq, 1.0, 0.0)
        seen = seen_ref[...]
        rank = _dot(earlier, eqf.astype(BF16)) + seen
        seen_ref[...] = seen + jnp.sum(eqf, axis=0, keepdims=True)
        sel = (key > thr) | (eq & (rank < ties_kept))
        if diag:
            sel = sel & causal_t
        bias = jnp.where(sel, 0.0, NEG).T
        s = _dot_nt(qa, sk_ref[pl.ds(off, t), :])
        s = jnp.concatenate([s[h * t:(h + 1) * t, :] + bias for h in range(DSA_HEADS)], axis=0)
        _store_logits(s, j, s_ref, mx_ref)

    _for_blocks(i, lambda j: logits_block(j, False))
    logits_block(i, True)
    _weighted_values(i + 1, t, s_ref, mx_ref, acc_ref, sv_ref, 1)
    o = _normalised(acc_ref[...])
    for c in range(DSA_HEADS // 2):
        lo = o[2 * c * t:(2 * c + 1) * t]
        hi = pltpu.roll(o[(2 * c + 1) * t:(2 * c + 2) * t], HEAD_DIM, 1)
        o_ref[:, c * LANES:(c + 1) * LANES] = jnp.where(upper, hi, lo).astype(BF16)


def _dsa(main, gates, *, batch, seq, t, topk):
    nq = seq // t
    nsq = DSA_HEADS // 2
    niq = IDX_HEADS // 2
    return pl.pallas_call(
        functools.partial(_dsa_kernel, t=t, topk=topk),
        grid=(batch, nq),
        in_specs=[
            pl.BlockSpec((t, niq * LANES), lambda b, i: (b * nq + i, C_IQ // niq)),
            pl.BlockSpec((seq, LANES), lambda b, i: (b, C_IK)),
            pl.BlockSpec((t, LANES), lambda b, i: (b * nq + i, 0)),
            pl.BlockSpec((t, nsq * LANES), lambda b, i: (b * nq + i, C_SQ // nsq)),
            pl.BlockSpec((seq, LANES), lambda b, i: (b, C_SK)),
            pl.BlockSpec((seq, LANES), lambda b, i: (b, C_SVA)),
        ],
        out_specs=pl.BlockSpec((t, nsq * LANES), lambda b, i: (b * nq + i, 0)),
        out_shape=jax.ShapeDtypeStruct((batch * seq, nsq * LANES), BF16),
        scratch_shapes=[
            pltpu.VMEM((nq, t, t), jnp.int32),
            pltpu.VMEM((nq, DSA_HEADS * t, t), F32),
            pltpu.VMEM((DSA_HEADS * t, LANES), F32),
            pltpu.VMEM((DSA_HEADS * t, LANES), F32),
            pltpu.VMEM((1, t), F32),
        ],
        compiler_params=pltpu.CompilerParams(vmem_limit_bytes=VMEM_LIMIT),
        name="dsa",
    )(main, main, gates, main, main, main)


def _ffn_kernel(x_ref, of_ref, od_ref, os_ref, wof_ref, wod_ref, wos_ref, gn_ref,
                wg_ref, wu_ref, wd_ref, o_ref, acc_ref):
    x1 = (x_ref[...] + _dot(of_ref[...], wof_ref[...]) + _dot(od_ref[...], wod_ref[...])
          + _dot(os_ref[...], wos_ref[...]))
    ms = jnp.mean(x1 * x1, axis=-1, keepdims=True)
    h = (x1 * lax.rsqrt(ms + NORM_EPS) * gn_ref[...]).astype(BF16)
    acc_ref[...] = x1

    def body(c, carry):
        gate = _dot(h, wg_ref[c])
        up = _dot(h, wu_ref[c])
        a = (gate * jax.nn.sigmoid(gate) * up).astype(BF16)
        acc_ref[...] += _dot(a, wd_ref[c])
        return carry

    lax.fori_loop(0, wg_ref.shape[0], body, 0)
    o_ref[...] = acc_ref[...]


def _ffn(x2, o_fox, o_diff, o_dsa, wof, wod, wos, gn, wg, wu, wd, *, tm):
    rows, d = x2.shape

    def const(a):
        nd = a.ndim
        return pl.BlockSpec(a.shape, lambda i: (0,) * nd, pipeline_mode=pl.Buffered(1))

    def rowblk(a):
        return pl.BlockSpec((tm, a.shape[1]), lambda i: (i, 0))

    return pl.pallas_call(
        _ffn_kernel,
        grid=(rows // tm,),
        in_specs=[rowblk(x2), rowblk(o_fox), rowblk(o_diff), rowblk(o_dsa),
                  const(wof), const(wod), const(wos), const(gn), const(wg), const(wu), const(wd)],
        out_specs=pl.BlockSpec((tm, d), lambda i: (i, 0)),
        out_shape=jax.ShapeDtypeStruct((rows, d), F32),
        scratch_shapes=[pltpu.VMEM((tm, d), F32)],
        compiler_params=pltpu.CompilerParams(vmem_limit_bytes=VMEM_LIMIT),
        name="ffn",
    )(x2, o_fox, o_diff, o_dsa, wof, wod, wos, gn, wg, wu, wd)


def _relayout_w_in(w):
    sizes = [FOX_HEADS * HEAD_DIM] * 3 + [FOX_HEADS] + [DIFF_HEADS * 2 * DIFF_QK_DIM] * 2 + \
            [DIFF_HEADS * DIFF_V_DIM, DSA_HEADS * HEAD_DIM, HEAD_DIM, HEAD_DIM,
             IDX_HEADS * HEAD_DIM, HEAD_DIM, IDX_HEADS]
    starts = np.concatenate([[0], np.cumsum(sizes)])
    fq, fk, fv, ff, dq, dk, dv, sq, sk, sv, iq, ik, iw = [w[:, int(starts[n]):int(starts[n + 1])]
                                                          for n in range(len(sizes))]
    pad = jnp.zeros((w.shape[0], LANES - FOX_HEADS - IDX_HEADS), w.dtype)
    w_all = jnp.concatenate([fq, fk, fv, dq, dk, dv, sq, sk, sk, sv, sv, iq, ik, ik, ff, iw, pad],
                            axis=1)
    assert w_all.shape[1] == N_PROJ * LANES
    return w_all.astype(BF16)


def _rope_tables(seq, head_dim):
    rot = head_dim // 4
    half = rot // 2
    inv_freq = 1.0 / (ROPE_THETA ** (jnp.arange(0, rot, 2, dtype=F32) / rot))
    ang = jnp.arange(seq, dtype=F32)[:, None] * inv_freq[None, :]
    cos, sin = jnp.cos(ang), jnp.sin(ang)
    r = np.arange(LANES) % head_dim
    f = r % half
    cos_t = jnp.where(r < rot, cos[:, f], 1.0)
    sin_hi = jnp.where((r >= half) & (r < rot), sin[:, f], 0.0)
    sin_lo = jnp.where(r < half, -sin[:, f], 0.0)
    return [cos_t, sin_hi, sin_lo]


def _pad_lanes(v):
    return jnp.pad(v.astype(F32), (0, LANES - v.shape[0]))


@jax.jit
def kernel(x, attn_norm, w_in, fox_fb, fox_qn, fox_kn, diff_qn, diff_kn, diff_lq1, diff_lk1,
           diff_lq2, diff_lk2, diff_subln, dsa_qn, dsa_kn, w_out, ffn_norm, w_gate_up, w_down):
    batch, seq, d = x.shape
    depth = w_in.shape[0]
    hidden = w_down.shape[1]
    topk = min(DSA_TOPK, seq // 4)
    t_att = 256
    tm = 512 if (batch * seq) % 512 == 0 else 256
    tm_in = min(tm, seq)
    ffn_chunk = 256
    assert hidden % ffn_chunk == 0 and seq % t_att == 0 and seq % tm_in == 0

    ropetab = jnp.stack(_rope_tables(seq, HEAD_DIM) + _rope_tables(seq, DIFF_QK_DIM))
    n_off = FOX_HEADS * HEAD_DIM
    n_od = DIFF_HEADS * DIFF_V_DIM

    x2 = x.reshape(batch * seq, d)
    for l in range(depth):
        w_all = _relayout_w_in(w_in[l])
        vec = jnp.stack([
            jnp.tile(fox_qn[l].astype(F32), 2) * (HEAD_DIM ** -0.5 * LOG2E),
            jnp.tile(fox_kn[l].astype(F32), 2),
            jnp.tile(diff_qn[l].astype(F32), 4) * (DIFF_QK_DIM ** -0.5 * LOG2E),
            jnp.tile(diff_kn[l].astype(F32), 4),
            jnp.tile(dsa_qn[l].astype(F32), 2) * (HEAD_DIM ** -0.5 * LOG2E),
            jnp.tile(dsa_kn[l].astype(F32), 2),
            _pad_lanes(fox_fb[l]),
            jnp.zeros((LANES,), F32),
        ])
        main, gates = _inproj(x2, attn_norm[l].astype(F32)[None, :], w_all, vec, ropetab,
                              seq=seq, tm=tm_in)
        ks, qs = _gates(gates, batch=batch, seq=seq, blk=t_att)
        o_fox = _fox(main, ks, qs, batch=batch, seq=seq, t=t_att)

        lam_init = 0.8 - 0.6 * math.exp(-0.3 * l)
        lam = (jnp.exp(jnp.sum(diff_lq1[l].astype(F32) * diff_lk1[l].astype(F32)))
               - jnp.exp(jnp.sum(diff_lq2[l].astype(F32) * diff_lk2[l].astype(F32))) + lam_init)
        gsub = (jnp.tile(diff_subln[l].astype(F32), 2) * (1.0 - lam_init))[None, :]
        o_diff = _diff(lam.reshape(1, 1), main, gsub, batch=batch, seq=seq, t=t_att)

        o_dsa = _dsa(main, gates, batch=batch, seq=seq, t=t_att, topk=topk)

        wo = w_out[l].astype(BF16)
        wgu = w_gate_up[l].astype(BF16)
        nch = hidden // ffn_chunk
        wg = wgu[:, :hidden].reshape(d, nch, ffn_chunk).transpose(1, 0, 2)
        wu = wgu[:, hidden:].reshape(d, nch, ffn_chunk).transpose(1, 0, 2)
        wd = w_down[l].astype(BF16).reshape(nch, ffn_chunk, d)
        x2 = _ffn(x2, o_fox, o_diff, o_dsa, wo[:n_off], wo[n_off:n_off + n_od], wo[n_off + n_od:],
                  ffn_norm[l].astype(F32)[None, :], wg, wu, wd, tm=tm)
    return x2.reshape(batch, seq, d)
```

```python
import functools
import math

import numpy as np
import jax
import jax.numpy as jnp
from jax import lax
from jax.experimental import pallas as pl
from jax.experimental.pallas import tpu as pltpu

F32 = jnp.float32
BF16 = jnp.bfloat16
LANES = 128
SUBLANES = 8
NORM_EPS = 1e-6
NEG = -1e30
INT_MIN = -(2 ** 31)
ROPE_THETA = 500000.0
LOG2E = 1.4426950408889634

HEAD_DIM = 64
FOX_HEADS = 6
DIFF_HEADS = 4
DIFF_QK_DIM = 32
DIFF_V_DIM = 64
DSA_HEADS = 6
IDX_HEADS = 4
DSA_TOPK = 256

P_FQ, P_FK, P_FV = 0, 3, 6
P_DQ, P_DK, P_DV = 9, 11, 13
P_SQ, P_SK, P_SV = 15, 18, 19
P_IQ, P_IK, P_GATES = 20, 22, 23
N_PROJ = 24
C_FQ, C_FK, C_SQ, C_SK = 0, 3, 6, 9
C_IQ, C_DQ, C_DK = 10, 12, 14
C_SVA, C_IK = 16, 17
C_FVA, C_DVA = 18, 24
N_MAIN = 28
CS_LANE0, CT_LANE0 = 0, 32

VMEM_LIMIT = 56 * 1024 * 1024


def _dot(a, b):
    return jnp.dot(a, b, preferred_element_type=F32)


def _dot_nt(a, b):
    return lax.dot_general(a, b, (((1,), (1,)), ((), ())), preferred_element_type=F32)


def _layer_block(a, layer):
    nd = a.ndim - 1
    return pl.BlockSpec((None,) + a.shape[1:], lambda *_: (layer,) + (0,) * nd,
                        pipeline_mode=pl.Buffered(1))


def _block_diag_ones(seg):
    r = lax.broadcasted_iota(jnp.int32, (LANES, LANES), 0) // seg
    c = lax.broadcasted_iota(jnp.int32, (LANES, LANES), 1) // seg
    return jnp.where(r == c, 1.0, 0.0).astype(BF16)


def _split3(v):
    hi = v.astype(BF16)
    r1 = v - hi.astype(F32)
    mid = r1.astype(BF16)
    lo = (r1 - mid.astype(F32)).astype(BF16)
    return hi, mid, lo


def _seg_sum(v, bd):
    hi = v.astype(BF16)
    lo = (v - hi.astype(F32)).astype(BF16)
    return _dot(hi, bd) + _dot(lo, bd)


def _inproj_kernel(x_ref, gn_ref, w_ref, vec_ref, rope_ref, main_ref, gates_ref, p_ref):
    x = x_ref[...]
    ms = jnp.mean(x * x, axis=-1, keepdims=True)
    h = (x * lax.rsqrt(ms + NORM_EPS) * gn_ref[...]).astype(BF16)
    p_ref[...] = _dot(h, w_ref[...])

    lane = lax.broadcasted_iota(jnp.int32, (x.shape[0], LANES), 1)
    upper = lane >= HEAD_DIM

    def chunk(c):
        return p_ref[:, c * LANES:(c + 1) * LANES]

    def put(c, y):
        main_ref[:, c * LANES:(c + 1) * LANES] = y.astype(BF16)

    def put_v_ones(c, y):
        put(c, jnp.where(upper, 1.0, y))
        put(c + 1, jnp.where(upper, 1.0, pltpu.roll(y, HEAD_DIM, 1)))

    def rope(y, tabs):
        if tabs is None:
            return y
        t0, sh = tabs
        return (y * rope_ref[t0] + pltpu.roll(y, sh, 1) * rope_ref[t0 + 1]
                + pltpu.roll(y, LANES - sh, 1) * rope_ref[t0 + 2])

    rope64, rope32 = (0, 8), (3, 4)
    bd64 = _block_diag_ones(HEAD_DIM)
    bd32 = _block_diag_ones(DIFF_QK_DIM)

    def segnorm(y, bd, seg, row):
        return y * lax.rsqrt(_seg_sum(y * y, bd) * (1.0 / seg) + NORM_EPS) * vec_ref[row:row + 1, :]

    for c in range(3):
        put(C_FQ + c, segnorm(chunk(P_FQ + c), bd64, HEAD_DIM, 0))
        put(C_FK + c, segnorm(chunk(P_FK + c), bd64, HEAD_DIM, 1))
        put_v_ones(C_FVA + 2 * c, chunk(P_FV + c))
        put(C_SQ + c, rope(segnorm(chunk(P_SQ + c), bd64, HEAD_DIM, 4), rope64))
    for c in range(2):
        put(C_DQ + c, rope(segnorm(chunk(P_DQ + c), bd32, DIFF_QK_DIM, 2), rope32))
        put(C_DK + c, rope(segnorm(chunk(P_DK + c), bd32, DIFF_QK_DIM, 3), rope32))
        put_v_ones(C_DVA + 2 * c, chunk(P_DV + c))
        put(C_IQ + c, rope(chunk(P_IQ + c), rope64))
    put(C_SK, rope(segnorm(chunk(P_SK), bd64, HEAD_DIM, 5), rope64))
    put(C_SVA, jnp.where(upper, 1.0, chunk(P_SV)))
    put(C_IK, rope(chunk(P_IK), rope64))

    z = chunk(P_GATES)
    v = z + vec_ref[6:7, :]
    logsig = jnp.minimum(v, 0.0) - jnp.log1p(jnp.exp(-jnp.abs(v)))
    gates_ref[...] = jnp.where(lane < FOX_HEADS, logsig, z * 0.0625)


def _inproj(x2, gn, w_all, vec, ropetab, *, layer, seq, tm):
    rows = x2.shape[0]
    nst = seq // tm
    return pl.pallas_call(
        _inproj_kernel,
        grid=(rows // tm,),
        in_specs=[
            pl.BlockSpec((tm, x2.shape[1]), lambda i: (i, 0)),
            _layer_block(gn, layer),
            _layer_block(w_all, layer),
            _layer_block(vec, layer),
            pl.BlockSpec((6, tm, LANES), lambda i: (0, i % nst, 0)),
        ],
        out_specs=[
            pl.BlockSpec((tm, N_MAIN * LANES), lambda i: (i, 0)),
            pl.BlockSpec((tm, LANES), lambda i: (i, 0)),
        ],
        out_shape=[
            jax.ShapeDtypeStruct((rows, N_MAIN * LANES), BF16),
            jax.ShapeDtypeStruct((rows, LANES), F32),
        ],
        scratch_shapes=[pltpu.VMEM((tm, N_PROJ * LANES), F32)],
        compiler_params=pltpu.CompilerParams(vmem_limit_bytes=VMEM_LIMIT),
        name="inproj",
    )(x2, gn, w_all, vec, ropetab)


def _gates_kernel(g_ref, ks_ref, qs_ref, *, blk):
    n = g_ref.shape[0] // blk
    r = lax.broadcasted_iota(jnp.int32, (blk, blk), 0)
    c = lax.broadcasted_iota(jnp.int32, (blk, blk), 1)
    tri = jnp.where(r >= c, 1.0, 0.0).astype(BF16)
    er = lax.broadcasted_iota(jnp.int32, (LANES, LANES), 0)
    ec = lax.broadcasted_iota(jnp.int32, (LANES, LANES), 1)
    head = er < FOX_HEADS

    def place(lane0, part, val):
        return jnp.where(head & (ec == lane0 + 3 * er + part), val, 0.0).astype(BF16)

    lane = lax.broadcasted_iota(jnp.int32, (blk, LANES), 1)
    k_ones = jnp.where((lane >= CT_LANE0) & (lane < CT_LANE0 + 3 * FOX_HEADS), 1.0, 0.0)
    carry = jnp.zeros((1, LANES), F32)
    for b in range(n):
        parts = _split3(g_ref[b * blk:(b + 1) * blk, :])
        cb = _dot(tri, parts[0]) + _dot(tri, parts[1]) + _dot(tri, parts[2]) + carry
        carry = cb[blk - 1:blk, :]
        c2 = _split3(cb * LOG2E)
        ks = k_ones
        qs = jnp.zeros((blk, LANES), F32)
        for part in range(3):
            ks = ks + _dot(c2[part], place(CS_LANE0, part, -1.0))
            qs = qs + _dot(c2[part], place(CT_LANE0, part, 1.0))
        ks_ref[b * blk:(b + 1) * blk, :] = ks.astype(BF16)
        qs_ref[b * blk:(b + 1) * blk, :] = qs.astype(BF16)


def _gates(gates, *, batch, seq, blk):
    return pl.pallas_call(
        functools.partial(_gates_kernel, blk=blk),
        grid=(batch,),
        in_specs=[pl.BlockSpec((seq, LANES), lambda b: (b, 0))],
        out_specs=[pl.BlockSpec((seq, LANES), lambda b: (b, 0)),
                   pl.BlockSpec((seq, LANES), lambda b: (b, 0))],
        out_shape=[jax.ShapeDtypeStruct((batch * seq, LANES), BF16),
                   jax.ShapeDtypeStruct((batch * seq, LANES), BF16)],
        name="gates",
    )(gates)


def _causal_rows(n, t):
    assert t & (t - 1) == 0
    row = lax.broadcasted_iota(jnp.int32, (n * t, t), 0) & (t - 1)
    col = lax.broadcasted_iota(jnp.int32, (n * t, t), 1)
    return col <= row


def _store_logits(js, logits_fn, s_ref, mx_ref, first=False):
    m = None if first else mx_ref[...]
    for j in js:
        s = logits_fn(j)
        s_ref[j] = s
        for c in range(s.shape[1] // LANES):
            sc = s[:, c * LANES:(c + 1) * LANES]
            m = sc if m is None else jnp.maximum(m, sc)
    mx_ref[...] = m


def _for_blocks(n, fn):
    def body(jj, carry):
        fn([4 * jj + u for u in range(4)])
        return carry

    lax.fori_loop(0, n // 4, body, 0)
    done4 = (n // 4) * 4

    @pl.when(n % 4 >= 2)
    def _():
        fn([done4, done4 + 1])

    @pl.when(n % 2 == 1)
    def _():
        fn([n - 1])


def _weighted_values(i, t, s_ref, mx_ref, acc_ref, v_ref, groups):
    m = jnp.max(mx_ref[...], axis=1, keepdims=True)
    rows = acc_ref.shape[0] // groups

    def pv(j):
        off = pl.multiple_of(j * t, t)
        p = jnp.exp2(s_ref[j] - m).astype(BF16)
        return [_dot(p[g * rows:(g + 1) * rows, :], v_ref[pl.ds(off, t), g * LANES:(g + 1) * LANES])
                for g in range(groups)]

    def add(js):
        parts = [pv(j) for j in js]
        for g in range(groups):
            acc_ref[g * rows:(g + 1) * rows, :] += functools.reduce(lambda a, b: a + b,
                                                                    [p[g] for p in parts])

    for g, part in enumerate(pv(i)):
        acc_ref[g * rows:(g + 1) * rows, :] = part
    _for_blocks(i, add)


def _normalised(acc):
    return acc / pltpu.roll(acc, HEAD_DIM, 1)


def _fox_kernel(q_ref, qs_ref, k_ref, ks_ref, v_ref, o_ref, s_ref, mx_ref, acc_ref, *, t):
    i = pl.program_id(1)
    npair = FOX_HEADS // 2
    qs = qs_ref[...].astype(F32)
    lane = lax.broadcasted_iota(jnp.int32, (t, LANES), 1)
    upper = lane >= HEAD_DIM
    zero = jnp.zeros((t, LANES), BF16)

    def side(h):
        cs0 = CS_LANE0 + 3 * h
        ct0 = CT_LANE0 + 3 * h
        pick = jnp.where((lane >= cs0) & (lane < cs0 + 3), 1.0, 0.0)
        return jnp.where((lane >= ct0) & (lane < ct0 + 3), qs, pick).astype(BF16)

    qa = []
    for p in range(npair):
        q = q_ref[:, p * LANES:(p + 1) * LANES]
        qa.append(jnp.concatenate([
            jnp.concatenate([jnp.where(upper, zero, q), side(2 * p)], axis=1),
            jnp.concatenate([jnp.where(upper, q, zero), side(2 * p + 1)], axis=1)], axis=0))

    def logits(j):
        off = pl.multiple_of(j * t, t)
        ksb = ks_ref[pl.ds(off, t), :]
        return jnp.concatenate([
            _dot_nt(qa[p], jnp.concatenate([k_ref[pl.ds(off, t), p * LANES:(p + 1) * LANES], ksb],
                                           axis=1))
            for p in range(npair)], axis=0)

    causal = _causal_rows(FOX_HEADS, t)
    _store_logits([i], lambda j: jnp.where(causal, logits(j), NEG), s_ref, mx_ref, first=True)
    _for_blocks(i, lambda js: _store_logits(js, logits, s_ref, mx_ref))
    _weighted_values(i, t, s_ref, mx_ref, acc_ref, v_ref, FOX_HEADS)
    o = _normalised(acc_ref[...])
    for p in range(npair):
        lo = o[2 * p * t:(2 * p + 1) * t]
        hi = pltpu.roll(o[(2 * p + 1) * t:(2 * p + 2) * t], HEAD_DIM, 1)
        o_ref[:, p * LANES:(p + 1) * LANES] = jnp.where(upper, hi, lo).astype(BF16)


def _fox(main, ks, qs, *, batch, seq, t):
    nq = seq // t
    npair = FOX_HEADS // 2
    return pl.pallas_call(
        functools.partial(_fox_kernel, t=t),
        grid=(batch, nq),
        in_specs=[
            pl.BlockSpec((t, npair * LANES), lambda b, i: (b * nq + i, C_FQ // npair)),
            pl.BlockSpec((t, LANES), lambda b, i: (b * nq + i, 0)),
            pl.BlockSpec((seq, npair * LANES), lambda b, i: (b, C_FK // npair)),
            pl.BlockSpec((seq, LANES), lambda b, i: (b, 0)),
            pl.BlockSpec((seq, FOX_HEADS * LANES), lambda b, i: (b, C_FVA // FOX_HEADS)),
        ],
        out_specs=pl.BlockSpec((t, npair * LANES), lambda b, i: (b * nq + i, 0)),
        out_shape=jax.ShapeDtypeStruct((batch * seq, npair * LANES), BF16),
        scratch_shapes=[
            pltpu.VMEM((nq, FOX_HEADS * t, t), F32),
            pltpu.VMEM((FOX_HEADS * t, LANES), F32),
            pltpu.VMEM((FOX_HEADS * t, LANES), F32),
        ],
        compiler_params=pltpu.CompilerParams(vmem_limit_bytes=VMEM_LIMIT),
        name="fox",
    )(main, qs, main, ks, main)


def _diff_kernel(lam_ref, q_ref, k_ref, v_ref, g_ref, o_ref, s_ref, mx_ref, acc_ref, *, t, layer):
    i = pl.program_id(1)
    npair = DIFF_HEADS // 2
    nmaps = 2 * DIFF_HEADS
    lane = lax.broadcasted_iota(jnp.int32, (t, LANES), 1)
    upper = lane >= DIFF_V_DIM
    zero = jnp.zeros((t, LANES), BF16)
    qa = []
    for p in range(npair):
        q = q_ref[:, p * LANES:(p + 1) * LANES]
        qa.append(jnp.concatenate([jnp.where((lane // DIFF_QK_DIM) == n, q, zero) for n in range(4)],
                                  axis=0))

    def logits(j):
        off = pl.multiple_of(j * t, t)
        return jnp.concatenate([_dot_nt(qa[p], k_ref[pl.ds(off, t), p * LANES:(p + 1) * LANES])
                                for p in range(npair)], axis=0)

    causal = _causal_rows(nmaps, t)
    _store_logits([i], lambda j: jnp.where(causal, logits(j), NEG), s_ref, mx_ref, first=True)
    _for_blocks(i, lambda js: _store_logits(js, logits, s_ref, mx_ref))
    _weighted_values(i, t, s_ref, mx_ref, acc_ref, v_ref, DIFF_HEADS)
    o = _normalised(acc_ref[...])
    lam = lam_ref[layer, 0]
    bd = _block_diag_ones(DIFF_V_DIM)
    for p in range(npair):
        heads = []
        for h in (2 * p, 2 * p + 1):
            heads.append(o[2 * h * t:(2 * h + 1) * t] - lam * o[(2 * h + 1) * t:(2 * h + 2) * t])
        y = jnp.where(upper, pltpu.roll(heads[1], DIFF_V_DIM, 1), heads[0])
        ss = _seg_sum(y * y, bd)
        y = y * lax.rsqrt(ss * (1.0 / DIFF_V_DIM) + NORM_EPS) * g_ref[...]
        o_ref[:, p * LANES:(p + 1) * LANES] = y.astype(BF16)


def _diff(lam, main, gsub, *, layer, batch, seq, t):
    nq = seq // t
    npair = DIFF_HEADS // 2
    nmaps = 2 * DIFF_HEADS
    return pl.pallas_call(
        functools.partial(_diff_kernel, t=t, layer=layer),
        grid=(batch, nq),
        in_specs=[
            pl.BlockSpec(memory_space=pltpu.SMEM),
            pl.BlockSpec((t, npair * LANES), lambda b, i: (b * nq + i, C_DQ // npair)),
            pl.BlockSpec((seq, npair * LANES), lambda b, i: (b, C_DK // npair)),
            pl.BlockSpec((seq, DIFF_HEADS * LANES), lambda b, i: (b, C_DVA // DIFF_HEADS)),
            _layer_block(gsub, layer),
        ],
        out_specs=pl.BlockSpec((t, npair * LANES), lambda b, i: (b * nq + i, 0)),
        out_shape=jax.ShapeDtypeStruct((batch * seq, npair * LANES), BF16),
        scratch_shapes=[
            pltpu.VMEM((nq, nmaps * t, t), F32),
            pltpu.VMEM((nmaps * t, LANES), F32),
            pltpu.VMEM((nmaps * t, LANES), F32),
        ],
        compiler_params=pltpu.CompilerParams(vmem_limit_bytes=VMEM_LIMIT),
        name="diff",
    )(lam, main, main, main, gsub)


def _dsa_kernel(iq_ref, ik_ref, g_ref, sq_ref, sk_ref, sv_ref, o_ref,
                key_ref, s_ref, mx_ref, acc_ref, seen_ref, *, t, topk):
    i = pl.program_id(1)
    lane = lax.broadcasted_iota(jnp.int32, (t, LANES), 1)
    upper = lane >= HEAD_DIM
    zero = jnp.zeros((t, LANES), BF16)
    kpos = lax.broadcasted_iota(jnp.int32, (t, t), 0)
    qpos = lax.broadcasted_iota(jnp.int32, (t, t), 1)
    causal_t = kpos <= qpos

    def head_rows(x, nheads):
        out = []
        for h in range(nheads):
            xc = x[:, (h // 2) * LANES:(h // 2 + 1) * LANES]
            out.append(jnp.where(upper, xc, zero) if h % 2 else jnp.where(upper, zero, xc))
        return out

    iqm = head_rows(iq_ref[...], IDX_HEADS)
    g_t = g_ref[...].T
    wts = [g_t[FOX_HEADS + h:FOX_HEADS + h + 1, :] for h in range(IDX_HEADS)]

    def score_block(j, diag):
        off = pl.multiple_of(j * t, t)
        kb = ik_ref[pl.ds(off, t), :]
        sc = wts[0] * jnp.maximum(_dot_nt(kb, iqm[0]), 0.0)
        for h in range(1, IDX_HEADS):
            sc = sc + wts[h] * jnp.maximum(_dot_nt(kb, iqm[h]), 0.0)
        bits = lax.bitcast_convert_type(sc, jnp.int32)
        key = jnp.where(bits < 0, jnp.int32(INT_MIN) - bits, bits)
        if diag:
            key = jnp.where(causal_t, key, INT_MIN)
        key_ref[j] = key

    _for_blocks(i, lambda js: [score_block(j, False) for j in js])
    score_block(i, True)

    @pl.when(i % 2 == 0)
    def _():
        key_ref[i + 1] = jnp.full((t, t), INT_MIN, jnp.int32)

    def count(pred_fn):
        def hits(j):
            hit = jnp.where(pred_fn(key_ref[j]), 1, 0)
            return jnp.sum(hit.reshape(t // SUBLANES, SUBLANES, t), axis=0)

        def cbody(jj, acc):
            return acc + hits(2 * jj) + hits(2 * jj + 1)

        acc = lax.fori_loop(0, i // 2 + 1, cbody, jnp.zeros((SUBLANES, t), jnp.int32))
        return jnp.sum(acc, axis=0, keepdims=True)

    def bit_body(it, carry):
        thr, n_ge = carry
        cand = thr + lax.shift_left(jnp.int32(1), 31 - it)
        cnt = count(lambda k: k >= cand)
        ok = cnt >= topk
        return jnp.where(ok, cand, thr), jnp.where(ok, cnt, n_ge)

    n_all = (i + 1) * t
    passes = jnp.where(n_all > topk, 32, 0)
    thr, n_ge = lax.fori_loop(0, passes, bit_body, (jnp.full((1, t), INT_MIN, jnp.int32),
                                                    jnp.full((1, t), n_all, jnp.int32)))
    n_gt = count(lambda k: k > thr)
    ties_kept = (topk - n_gt).astype(F32)
    ties_all = (n_ge - n_gt).astype(F32)

    qa = jnp.concatenate(head_rows(sq_ref[...], DSA_HEADS), axis=0)
    earlier = jnp.where(qpos < kpos, 1.0, 0.0).astype(BF16)
    seen_ref[...] = jnp.zeros(seen_ref.shape, F32)

    def masked_logits(j, diag):
        off = pl.multiple_of(j * t, t)
        key = key_ref[j]
        eq = key == thr
        eqf = jnp.where(eq, 1.0, 0.0)
        ties_here = jnp.sum(eqf, axis=0, keepdims=True)
        if diag:
            seen = ties_all - ties_here
        else:
            seen = seen_ref[...]
            seen_ref[...] = seen + ties_here
        rank = _dot(earlier, eqf.astype(BF16)) + seen
        sel = (key > thr) | (eq & (rank < ties_kept))
        if diag:
            sel = sel & causal_t
        bias = jnp.where(sel, 0.0, NEG).T
        s = _dot_nt(qa, sk_ref[pl.ds(off, t), :])
        return jnp.concatenate([s[h * t:(h + 1) * t, :] + bias for h in range(DSA_HEADS)], axis=0)

    _store_logits([i], lambda j: masked_logits(j, True), s_ref, mx_ref, first=True)
    _for_blocks(i, lambda js: _store_logits(js, lambda j: masked_logits(j, False), s_ref, mx_ref))
    _weighted_values(i, t, s_ref, mx_ref, acc_ref, sv_ref, 1)
    o = _normalised(acc_ref[...])
    for c in range(DSA_HEADS // 2):
        lo = o[2 * c * t:(2 * c + 1) * t]
        hi = pltpu.roll(o[(2 * c + 1) * t:(2 * c + 2) * t], HEAD_DIM, 1)
        o_ref[:, c * LANES:(c + 1) * LANES] = jnp.where(upper, hi, lo).astype(BF16)


def _dsa(main, gates, *, batch, seq, t, topk):
    nq = seq // t
    nsq = DSA_HEADS // 2
    niq = IDX_HEADS // 2
    return pl.pallas_call(
        functools.partial(_dsa_kernel, t=t, topk=topk),
        grid=(batch, nq),
        in_specs=[
            pl.BlockSpec((t, niq * LANES), lambda b, i: (b * nq + i, C_IQ // niq)),
            pl.BlockSpec((seq, LANES), lambda b, i: (b, C_IK)),
            pl.BlockSpec((t, LANES), lambda b, i: (b * nq + i, 0)),
            pl.BlockSpec((t, nsq * LANES), lambda b, i: (b * nq + i, C_SQ // nsq)),
            pl.BlockSpec((seq, LANES), lambda b, i: (b, C_SK)),
            pl.BlockSpec((seq, LANES), lambda b, i: (b, C_SVA)),
        ],
        out_specs=pl.BlockSpec((t, nsq * LANES), lambda b, i: (b * nq + i, 0)),
        out_shape=jax.ShapeDtypeStruct((batch * seq, nsq * LANES), BF16),
        scratch_shapes=[
            pltpu.VMEM((nq + 1, t, t), jnp.int32),
            pltpu.VMEM((nq, DSA_HEADS * t, t), F32),
            pltpu.VMEM((DSA_HEADS * t, LANES), F32),
            pltpu.VMEM((DSA_HEADS * t, LANES), F32),
            pltpu.VMEM((1, t), F32),
        ],
        compiler_params=pltpu.CompilerParams(vmem_limit_bytes=VMEM_LIMIT),
        name="dsa",
    )(main, main, gates, main, main, main)


def _ffn_kernel(x_ref, of_ref, od_ref, os_ref, wo_ref, gn_ref, wgu_ref, wd_ref, o_ref, acc_ref,
                *, chunk):
    mixed = jnp.concatenate([of_ref[...], od_ref[...], os_ref[...]], axis=1)
    x1 = x_ref[...] + _dot(mixed, wo_ref[...])
    ms = jnp.mean(x1 * x1, axis=-1, keepdims=True)
    h = (x1 * lax.rsqrt(ms + NORM_EPS) * gn_ref[...]).astype(BF16)
    acc_ref[...] = x1
    hidden = wd_ref.shape[0]
    for c0 in range(0, hidden, chunk):
        gate = _dot(h, wgu_ref[:, c0:c0 + chunk])
        up = _dot(h, wgu_ref[:, hidden + c0:hidden + c0 + chunk])
        a = (gate * jax.nn.sigmoid(gate) * up).astype(BF16)
        acc_ref[...] += _dot(a, wd_ref[c0:c0 + chunk, :])
    o_ref[...] = acc_ref[...]


def _ffn(x2, o_fox, o_diff, o_dsa, wo, gn, wgu, wd, *, layer, tm, chunk):
    rows, d = x2.shape

    def rowblk(a):
        return pl.BlockSpec((tm, a.shape[1]), lambda i: (i, 0))

    return pl.pallas_call(
        functools.partial(_ffn_kernel, chunk=chunk),
        grid=(rows // tm,),
        in_specs=[rowblk(x2), rowblk(o_fox), rowblk(o_diff), rowblk(o_dsa),
                  _layer_block(wo, layer), _layer_block(gn, layer), _layer_block(wgu, layer),
                  _layer_block(wd, layer)],
        out_specs=pl.BlockSpec((tm, d), lambda i: (i, 0)),
        out_shape=jax.ShapeDtypeStruct((rows, d), F32),
        scratch_shapes=[pltpu.VMEM((tm, d), F32)],
        compiler_params=pltpu.CompilerParams(vmem_limit_bytes=VMEM_LIMIT),
        name="ffn",
    )(x2, o_fox, o_diff, o_dsa, wo, gn, wgu, wd)


def _relayout_w_in(w):
    sizes = [FOX_HEADS * HEAD_DIM] * 3 + [FOX_HEADS] + [DIFF_HEADS * 2 * DIFF_QK_DIM] * 2 + \
            [DIFF_HEADS * DIFF_V_DIM, DSA_HEADS * HEAD_DIM, HEAD_DIM, HEAD_DIM,
             IDX_HEADS * HEAD_DIM, HEAD_DIM, IDX_HEADS]
    starts = np.concatenate([[0], np.cumsum(sizes)])
    fq, fk, fv, ff, dq, dk, dv, sq, sk, sv, iq, ik, iw = [w[..., int(starts[n]):int(starts[n + 1])]
                                                          for n in range(len(sizes))]
    pad = jnp.zeros(w.shape[:-1] + (LANES - FOX_HEADS - IDX_HEADS,), w.dtype)
    w_all = jnp.concatenate([fq, fk, fv, dq, dk, dv, sq, sk, sk, sv, sv, iq, ik, ik, ff, iw, pad],
                            axis=-1)
    assert w_all.shape[-1] == N_PROJ * LANES
    return w_all.astype(BF16)


def _rope_tables(seq, head_dim):
    rot = head_dim // 4
    half = rot // 2
    inv_freq = 1.0 / (ROPE_THETA ** (jnp.arange(0, rot, 2, dtype=F32) / rot))
    ang = jnp.arange(seq, dtype=F32)[:, None] * inv_freq[None, :]
    cos, sin = jnp.cos(ang), jnp.sin(ang)
    r = np.arange(LANES) % head_dim
    f = r % half
    cos_t = jnp.where(r < rot, cos[:, f], 1.0)
    sin_hi = jnp.where((r >= half) & (r < rot), sin[:, f], 0.0)
    sin_lo = jnp.where(r < half, -sin[:, f], 0.0)
    return [cos_t, sin_hi, sin_lo]


def _pad_lanes(v):
    return jnp.pad(v.astype(F32), ((0, 0), (0, LANES - v.shape[1])))


@jax.jit
def kernel(x, attn_norm, w_in, fox_fb, fox_qn, fox_kn, diff_qn, diff_kn, diff_lq1, diff_lk1,
           diff_lq2, diff_lk2, diff_subln, dsa_qn, dsa_kn, w_out, ffn_norm, w_gate_up, w_down):
    batch, seq, d = x.shape
    depth = w_in.shape[0]
    hidden = w_down.shape[1]
    topk = min(DSA_TOPK, seq // 4)
    t_att = 256
    tm = 512 if (batch * seq) % 512 == 0 else 256
    tm_in = min(tm, seq)
    ffn_chunk = 256
    assert hidden % ffn_chunk == 0 and seq % t_att == 0 and seq % tm_in == 0

    ropetab = jnp.stack(_rope_tables(seq, HEAD_DIM) + _rope_tables(seq, DIFF_QK_DIM))

    w_all = _relayout_w_in(w_in)
    wo = w_out.astype(BF16)
    wgu = w_gate_up.astype(BF16)
    wd = w_down.astype(BF16)
    gn_attn = attn_norm.astype(F32)[:, None, :]
    gn_ffn = ffn_norm.astype(F32)[:, None, :]
    vec = jnp.stack([
        jnp.tile(fox_qn.astype(F32), (1, 2)) * (HEAD_DIM ** -0.5 * LOG2E),
        jnp.tile(fox_kn.astype(F32), (1, 2)),
        jnp.tile(diff_qn.astype(F32), (1, 4)) * (DIFF_QK_DIM ** -0.5 * LOG2E),
        jnp.tile(diff_kn.astype(F32), (1, 4)),
        jnp.tile(dsa_qn.astype(F32), (1, 2)) * (HEAD_DIM ** -0.5 * LOG2E),
        jnp.tile(dsa_kn.astype(F32), (1, 2)),
        _pad_lanes(fox_fb),
        jnp.zeros((depth, LANES), F32),
    ], axis=1)
    lam_init = jnp.asarray([0.8 - 0.6 * math.exp(-0.3 * l) for l in range(depth)], F32)
    lam = (jnp.exp(jnp.sum(diff_lq1.astype(F32) * diff_lk1.astype(F32), axis=-1))
           - jnp.exp(jnp.sum(diff_lq2.astype(F32) * diff_lk2.astype(F32), axis=-1)) + lam_init)
    lam = lam[:, None]
    gsub = (jnp.tile(diff_subln.astype(F32), (1, 2)) * (1.0 - lam_init)[:, None])[:, None, :]

    x2 = x.reshape(batch * seq, d)
    for l in range(depth):
        main, gates = _inproj(x2, gn_attn, w_all, vec, ropetab, layer=l, seq=seq, tm=tm_in)
        ks, qs = _gates(gates, batch=batch, seq=seq, blk=t_att)
        o_fox = _fox(main, ks, qs, batch=batch, seq=seq, t=t_att)
        o_diff = _diff(lam, main, gsub, layer=l, batch=batch, seq=seq, t=t_att)
        o_dsa = _dsa(main, gates, batch=batch, seq=seq, t=t_att, topk=topk)
        x2 = _ffn(x2, o_fox, o_diff, o_dsa, wo, gn_ffn, wgu, wd, layer=l, tm=tm, chunk=ffn_chunk)
    return x2.reshape(batch, seq, d)
```

```python
import functools
import math

import numpy as np
import jax
import jax.numpy as jnp
from jax import lax
from jax.experimental import pallas as pl
from jax.experimental.pallas import tpu as pltpu

F32 = jnp.float32
BF16 = jnp.bfloat16
LANES = 128
SUBLANES = 8
NORM_EPS = 1e-6
NEG = -1e30
INT_MIN = -(2 ** 31)
ROPE_THETA = 500000.0
LOG2E = 1.4426950408889634

HEAD_DIM = 64
FOX_HEADS = 6
DIFF_HEADS = 4
DIFF_QK_DIM = 32
DIFF_V_DIM = 64
DSA_HEADS = 6
IDX_HEADS = 4
DSA_TOPK = 256

P_FQ, P_FK, P_FV = 0, 3, 6
P_DQ, P_DK, P_DV = 9, 11, 13
P_SQ, P_SK, P_SV = 15, 18, 19
P_IQ, P_IK, P_GATES = 20, 22, 23
N_PROJ = 24
C_FQ, C_FK, C_SQ, C_SK = 0, 3, 6, 9
C_IQ, C_DQ, C_DK = 10, 12, 14
C_SVA, C_IK = 16, 17
C_FVA, C_DVA = 18, 24
N_MAIN = 28
CS_LANE0, CT_LANE0 = 0, 32

VMEM_LIMIT = 56 * 1024 * 1024


def _dot(a, b):
    return jnp.dot(a, b, preferred_element_type=F32)


def _dot_nt(a, b):
    return lax.dot_general(a, b, (((1,), (1,)), ((), ())), preferred_element_type=F32)


def _layer_block(a, layer):
    nd = a.ndim - 1
    return pl.BlockSpec((None,) + a.shape[1:], lambda *_: (layer,) + (0,) * nd,
                        pipeline_mode=pl.Buffered(1))


def _block_diag_ones(seg):
    r = lax.broadcasted_iota(jnp.int32, (LANES, LANES), 0) // seg
    c = lax.broadcasted_iota(jnp.int32, (LANES, LANES), 1) // seg
    return jnp.where(r == c, 1.0, 0.0).astype(BF16)


def _split3(v):
    hi = v.astype(BF16)
    r1 = v - hi.astype(F32)
    mid = r1.astype(BF16)
    lo = (r1 - mid.astype(F32)).astype(BF16)
    return hi, mid, lo


def _seg_sum(v, bd):
    hi = v.astype(BF16)
    lo = (v - hi.astype(F32)).astype(BF16)
    return _dot(hi, bd) + _dot(lo, bd)


def _inproj_kernel(x_ref, gn_ref, w_ref, vec_ref, rope_ref, main_ref, gates_ref, p_ref):
    x = x_ref[...]
    ms = jnp.mean(x * x, axis=-1, keepdims=True)
    h = (x * lax.rsqrt(ms + NORM_EPS) * gn_ref[...]).astype(BF16)
    p_ref[...] = _dot(h, w_ref[...])

    lane = lax.broadcasted_iota(jnp.int32, (x.shape[0], LANES), 1)
    upper = lane >= HEAD_DIM

    def chunk(c):
        return p_ref[:, c * LANES:(c + 1) * LANES]

    def put(c, y):
        main_ref[:, c * LANES:(c + 1) * LANES] = y.astype(BF16)

    def put_v_ones(c, y):
        put(c, jnp.where(upper, 1.0, y))
        put(c + 1, jnp.where(upper, 1.0, pltpu.roll(y, HEAD_DIM, 1)))

    def rope(y, tabs):
        if tabs is None:
            return y
        t0, sh = tabs
        return (y * rope_ref[t0] + pltpu.roll(y, sh, 1) * rope_ref[t0 + 1]
                + pltpu.roll(y, LANES - sh, 1) * rope_ref[t0 + 2])

    rope64, rope32 = (0, 8), (3, 4)
    bd64 = _block_diag_ones(HEAD_DIM)
    bd32 = _block_diag_ones(DIFF_QK_DIM)

    def segnorm(y, bd, seg, row):
        return y * lax.rsqrt(_seg_sum(y * y, bd) * (1.0 / seg) + NORM_EPS) * vec_ref[row:row + 1, :]

    for c in range(3):
        put(C_FQ + c, segnorm(chunk(P_FQ + c), bd64, HEAD_DIM, 0))
        put(C_FK + c, segnorm(chunk(P_FK + c), bd64, HEAD_DIM, 1))
        put_v_ones(C_FVA + 2 * c, chunk(P_FV + c))
        put(C_SQ + c, rope(segnorm(chunk(P_SQ + c), bd64, HEAD_DIM, 4), rope64))
    for c in range(2):
        put(C_DQ + c, rope(segnorm(chunk(P_DQ + c), bd32, DIFF_QK_DIM, 2), rope32))
        put(C_DK + c, rope(segnorm(chunk(P_DK + c), bd32, DIFF_QK_DIM, 3), rope32))
        put_v_ones(C_DVA + 2 * c, chunk(P_DV + c))
        put(C_IQ + c, rope(chunk(P_IQ + c), rope64))
    put(C_SK, rope(segnorm(chunk(P_SK), bd64, HEAD_DIM, 5), rope64))
    put(C_SVA, jnp.where(upper, 1.0, chunk(P_SV)))
    put(C_IK, rope(chunk(P_IK), rope64))

    z = chunk(P_GATES)
    v = z + vec_ref[6:7, :]
    logsig = jnp.minimum(v, 0.0) - jnp.log1p(jnp.exp(-jnp.abs(v)))
    gates_ref[...] = jnp.where(lane < FOX_HEADS, logsig, z * 0.0625)


def _inproj(x2, gn, w_all, vec, ropetab, *, layer, seq, tm):
    rows = x2.shape[0]
    nst = seq // tm
    return pl.pallas_call(
        _inproj_kernel,
        grid=(rows // tm,),
        in_specs=[
            pl.BlockSpec((tm, x2.shape[1]), lambda i: (i, 0)),
            _layer_block(gn, layer),
            _layer_block(w_all, layer),
            _layer_block(vec, layer),
            pl.BlockSpec((6, tm, LANES), lambda i: (0, i % nst, 0)),
        ],
        out_specs=[
            pl.BlockSpec((tm, N_MAIN * LANES), lambda i: (i, 0)),
            pl.BlockSpec((tm, LANES), lambda i: (i, 0)),
        ],
        out_shape=[
            jax.ShapeDtypeStruct((rows, N_MAIN * LANES), BF16),
            jax.ShapeDtypeStruct((rows, LANES), F32),
        ],
        scratch_shapes=[pltpu.VMEM((tm, N_PROJ * LANES), F32)],
        compiler_params=pltpu.CompilerParams(vmem_limit_bytes=VMEM_LIMIT),
        name="inproj",
    )(x2, gn, w_all, vec, ropetab)


def _gates_kernel(g_ref, ks_ref, qs_ref, *, blk):
    n = g_ref.shape[0] // blk
    r = lax.broadcasted_iota(jnp.int32, (blk, blk), 0)
    c = lax.broadcasted_iota(jnp.int32, (blk, blk), 1)
    tri = jnp.where(r >= c, 1.0, 0.0).astype(BF16)
    er = lax.broadcasted_iota(jnp.int32, (LANES, LANES), 0)
    ec = lax.broadcasted_iota(jnp.int32, (LANES, LANES), 1)
    head = er < FOX_HEADS

    def place(lane0, part, val):
        return jnp.where(head & (ec == lane0 + 3 * er + part), val, 0.0).astype(BF16)

    lane = lax.broadcasted_iota(jnp.int32, (blk, LANES), 1)
    k_ones = jnp.where((lane >= CT_LANE0) & (lane < CT_LANE0 + 3 * FOX_HEADS), 1.0, 0.0)
    carry = jnp.zeros((1, LANES), F32)
    for b in range(n):
        parts = _split3(g_ref[b * blk:(b + 1) * blk, :])
        cb = _dot(tri, parts[0]) + _dot(tri, parts[1]) + _dot(tri, parts[2]) + carry
        carry = cb[blk - 1:blk, :]
        c2 = _split3(cb * LOG2E)
        ks = k_ones
        qs = jnp.zeros((blk, LANES), F32)
        for part in range(3):
            ks = ks + _dot(c2[part], place(CS_LANE0, part, -1.0))
            qs = qs + _dot(c2[part], place(CT_LANE0, part, 1.0))
        ks_ref[b * blk:(b + 1) * blk, :] = ks.astype(BF16)
        qs_ref[b * blk:(b + 1) * blk, :] = qs.astype(BF16)


def _gates(gates, *, batch, seq, blk):
    return pl.pallas_call(
        functools.partial(_gates_kernel, blk=blk),
        grid=(batch,),
        in_specs=[pl.BlockSpec((seq, LANES), lambda b: (b, 0))],
        out_specs=[pl.BlockSpec((seq, LANES), lambda b: (b, 0)),
                   pl.BlockSpec((seq, LANES), lambda b: (b, 0))],
        out_shape=[jax.ShapeDtypeStruct((batch * seq, LANES), BF16),
                   jax.ShapeDtypeStruct((batch * seq, LANES), BF16)],
        name="gates",
    )(gates)


def _causal_rows(n, t):
    assert t & (t - 1) == 0
    row = lax.broadcasted_iota(jnp.int32, (n * t, t), 0) & (t - 1)
    col = lax.broadcasted_iota(jnp.int32, (n * t, t), 1)
    return col <= row


def _store_logits(blocks, s_ref, mx_ref, first=False):
    m = None if first else mx_ref[...]
    for j, logits in blocks:
        s = logits()
        s_ref[j] = s
        for c in range(s.shape[1] // LANES):
            sc = s[:, c * LANES:(c + 1) * LANES]
            m = sc if m is None else jnp.maximum(m, sc)
    mx_ref[...] = m


def _blocks_diag_first(i, fn):
    fn([i], True)
    _for_blocks(i, lambda js: fn(js, False))


def _causal_pass1(logits, causal, s_ref, mx_ref, js, diag):
    blocks = [(j, functools.partial(logits, j)) for j in js]
    if diag:
        j = js[-1]
        blocks[-1] = (j, lambda: jnp.where(causal, logits(j), NEG))
    _store_logits(blocks, s_ref, mx_ref, first=diag)


def _for_blocks(n, fn):
    def body(jj, carry):
        fn([4 * jj + u for u in range(4)])
        return carry

    lax.fori_loop(0, n // 4, body, 0)
    done4 = (n // 4) * 4

    @pl.when(n % 4 >= 2)
    def _():
        fn([done4, done4 + 1])

    @pl.when(n % 2 == 1)
    def _():
        fn([n - 1])


def _weighted_values(i, t, s_ref, mx_ref, acc_ref, v_ref, groups):
    m = jnp.max(mx_ref[...], axis=1, keepdims=True)
    rows = acc_ref.shape[0] // groups

    def pv(j):
        off = pl.multiple_of(j * t, t)
        p = jnp.exp2(s_ref[j] - m).astype(BF16)
        return [_dot(p[g * rows:(g + 1) * rows, :], v_ref[pl.ds(off, t), g * LANES:(g + 1) * LANES])
                for g in range(groups)]

    def add(js, first):
        parts = [pv(j) for j in js]
        for g in range(groups):
            total = functools.reduce(lambda a, b: a + b, [p[g] for p in parts])
            if first:
                acc_ref[g * rows:(g + 1) * rows, :] = total
            else:
                acc_ref[g * rows:(g + 1) * rows, :] += total

    _blocks_diag_first(i, add)


def _normalised(acc):
    return acc / pltpu.roll(acc, HEAD_DIM, 1)


def _fox_kernel(q_ref, qs_ref, k_ref, ks_ref, v_ref, o_ref, s_ref, mx_ref, acc_ref, *, t):
    i = pl.program_id(1)
    npair = FOX_HEADS // 2
    qs = qs_ref[...].astype(F32)
    lane = lax.broadcasted_iota(jnp.int32, (t, LANES), 1)
    upper = lane >= HEAD_DIM
    zero = jnp.zeros((t, LANES), BF16)

    def side(h):
        cs0 = CS_LANE0 + 3 * h
        ct0 = CT_LANE0 + 3 * h
        pick = jnp.where((lane >= cs0) & (lane < cs0 + 3), 1.0, 0.0)
        return jnp.where((lane >= ct0) & (lane < ct0 + 3), qs, pick).astype(BF16)

    qa = []
    for p in range(npair):
        q = q_ref[:, p * LANES:(p + 1) * LANES]
        qa.append(jnp.concatenate([
            jnp.concatenate([jnp.where(upper, zero, q), side(2 * p)], axis=1),
            jnp.concatenate([jnp.where(upper, q, zero), side(2 * p + 1)], axis=1)], axis=0))

    def logits(j):
        off = pl.multiple_of(j * t, t)
        ksb = ks_ref[pl.ds(off, t), :]
        return jnp.concatenate([
            _dot_nt(qa[p], jnp.concatenate([k_ref[pl.ds(off, t), p * LANES:(p + 1) * LANES], ksb],
                                           axis=1))
            for p in range(npair)], axis=0)

    _blocks_diag_first(i, functools.partial(_causal_pass1, logits, _causal_rows(FOX_HEADS, t),
                                            s_ref, mx_ref))
    _weighted_values(i, t, s_ref, mx_ref, acc_ref, v_ref, FOX_HEADS)
    o = _normalised(acc_ref[...])
    for p in range(npair):
        lo = o[2 * p * t:(2 * p + 1) * t]
        hi = pltpu.roll(o[(2 * p + 1) * t:(2 * p + 2) * t], HEAD_DIM, 1)
        o_ref[:, p * LANES:(p + 1) * LANES] = jnp.where(upper, hi, lo).astype(BF16)


def _fox(main, ks, qs, *, batch, seq, t):
    nq = seq // t
    npair = FOX_HEADS // 2
    return pl.pallas_call(
        functools.partial(_fox_kernel, t=t),
        grid=(batch, nq),
        in_specs=[
            pl.BlockSpec((t, npair * LANES), lambda b, i: (b * nq + i, C_FQ // npair)),
            pl.BlockSpec((t, LANES), lambda b, i: (b * nq + i, 0)),
            pl.BlockSpec((seq, npair * LANES), lambda b, i: (b, C_FK // npair)),
            pl.BlockSpec((seq, LANES), lambda b, i: (b, 0)),
            pl.BlockSpec((seq, FOX_HEADS * LANES), lambda b, i: (b, C_FVA // FOX_HEADS)),
        ],
        out_specs=pl.BlockSpec((t, npair * LANES), lambda b, i: (b * nq + i, 0)),
        out_shape=jax.ShapeDtypeStruct((batch * seq, npair * LANES), BF16),
        scratch_shapes=[
            pltpu.VMEM((nq, FOX_HEADS * t, t), F32),
            pltpu.VMEM((FOX_HEADS * t, LANES), F32),
            pltpu.VMEM((FOX_HEADS * t, LANES), F32),
        ],
        compiler_params=pltpu.CompilerParams(vmem_limit_bytes=VMEM_LIMIT),
        name="fox",
    )(main, qs, main, ks, main)


def _diff_kernel(lam_ref, q_ref, k_ref, v_ref, g_ref, o_ref, s_ref, mx_ref, acc_ref, *, t, layer):
    i = pl.program_id(1)
    npair = DIFF_HEADS // 2
    nmaps = 2 * DIFF_HEADS
    lane = lax.broadcasted_iota(jnp.int32, (t, LANES), 1)
    upper = lane >= DIFF_V_DIM
    zero = jnp.zeros((t, LANES), BF16)
    qa = []
    for p in range(npair):
        q = q_ref[:, p * LANES:(p + 1) * LANES]
        qa.append(jnp.concatenate([jnp.where((lane // DIFF_QK_DIM) == n, q, zero) for n in range(4)],
                                  axis=0))

    def logits(j):
        off = pl.multiple_of(j * t, t)
        return jnp.concatenate([_dot_nt(qa[p], k_ref[pl.ds(off, t), p * LANES:(p + 1) * LANES])
                                for p in range(npair)], axis=0)

    _blocks_diag_first(i, functools.partial(_causal_pass1, logits, _causal_rows(nmaps, t),
                                            s_ref, mx_ref))
    _weighted_values(i, t, s_ref, mx_ref, acc_ref, v_ref, DIFF_HEADS)
    o = _normalised(acc_ref[...])
    lam = lam_ref[layer, 0]
    bd = _block_diag_ones(DIFF_V_DIM)
    for p in range(npair):
        heads = []
        for h in (2 * p, 2 * p + 1):
            heads.append(o[2 * h * t:(2 * h + 1) * t] - lam * o[(2 * h + 1) * t:(2 * h + 2) * t])
        y = jnp.where(upper, pltpu.roll(heads[1], DIFF_V_DIM, 1), heads[0])
        ss = _seg_sum(y * y, bd)
        y = y * lax.rsqrt(ss * (1.0 / DIFF_V_DIM) + NORM_EPS) * g_ref[...]
        o_ref[:, p * LANES:(p + 1) * LANES] = y.astype(BF16)


def _diff(lam, main, gsub, *, layer, batch, seq, t):
    nq = seq // t
    npair = DIFF_HEADS // 2
    nmaps = 2 * DIFF_HEADS
    return pl.pallas_call(
        functools.partial(_diff_kernel, t=t, layer=layer),
        grid=(batch, nq),
        in_specs=[
            pl.BlockSpec(memory_space=pltpu.SMEM),
            pl.BlockSpec((t, npair * LANES), lambda b, i: (b * nq + i, C_DQ // npair)),
            pl.BlockSpec((seq, npair * LANES), lambda b, i: (b, C_DK // npair)),
            pl.BlockSpec((seq, DIFF_HEADS * LANES), lambda b, i: (b, C_DVA // DIFF_HEADS)),
            _layer_block(gsub, layer),
        ],
        out_specs=pl.BlockSpec((t, npair * LANES), lambda b, i: (b * nq + i, 0)),
        out_shape=jax.ShapeDtypeStruct((batch * seq, npair * LANES), BF16),
        scratch_shapes=[
            pltpu.VMEM((nq, nmaps * t, t), F32),
            pltpu.VMEM((nmaps * t, LANES), F32),
            pltpu.VMEM((nmaps * t, LANES), F32),
        ],
        compiler_params=pltpu.CompilerParams(vmem_limit_bytes=VMEM_LIMIT),
        name="diff",
    )(lam, main, main, main, gsub)


def _dsa_kernel(iq_ref, ik_ref, g_ref, sq_ref, sk_ref, sv_ref, o_ref,
                key_ref, s_ref, mx_ref, acc_ref, seen_ref, *, t, topk):
    i = pl.program_id(1)
    lane = lax.broadcasted_iota(jnp.int32, (t, LANES), 1)
    upper = lane >= HEAD_DIM
    zero = jnp.zeros((t, LANES), BF16)
    kpos = lax.broadcasted_iota(jnp.int32, (t, t), 0)
    qpos = lax.broadcasted_iota(jnp.int32, (t, t), 1)
    causal_t = kpos <= qpos

    def head_rows(x, nheads):
        out = []
        for h in range(nheads):
            xc = x[:, (h // 2) * LANES:(h // 2 + 1) * LANES]
            out.append(jnp.where(upper, xc, zero) if h % 2 else jnp.where(upper, zero, xc))
        return out

    iqm = head_rows(iq_ref[...], IDX_HEADS)
    g_t = g_ref[...].T
    wts = [g_t[FOX_HEADS + h:FOX_HEADS + h + 1, :] for h in range(IDX_HEADS)]

    def score_block(j, diag):
        off = pl.multiple_of(j * t, t)
        kb = ik_ref[pl.ds(off, t), :]
        sc = wts[0] * jnp.maximum(_dot_nt(kb, iqm[0]), 0.0)
        for h in range(1, IDX_HEADS):
            sc = sc + wts[h] * jnp.maximum(_dot_nt(kb, iqm[h]), 0.0)
        bits = lax.bitcast_convert_type(sc, jnp.int32)
        key = jnp.where(bits < 0, jnp.int32(INT_MIN) - bits, bits)
        if diag:
            key = jnp.where(causal_t, key, INT_MIN)
        key_ref[j] = key

    _for_blocks(i, lambda js: [score_block(j, False) for j in js])
    score_block(i, True)

    @pl.when(i % 2 == 0)
    def _():
        key_ref[i + 1] = jnp.full((t, t), INT_MIN, jnp.int32)

    lanes_of_acc = 4
    grp = t // (SUBLANES * lanes_of_acc)

    def count(pred_fn):
        def bump(j, acc):
            pred = pred_fn(key_ref[j])
            rows = lanes_of_acc * SUBLANES
            for g in range(grp):
                acc = jnp.where(pred[g * rows:(g + 1) * rows, :], acc + 1, acc)
            return acc

        def cbody(jj, acc):
            return bump(2 * jj + 1, bump(2 * jj, acc))

        acc = lax.fori_loop(0, i // 2 + 1, cbody,
                            jnp.zeros((lanes_of_acc * SUBLANES, t), jnp.int32))
        return jnp.sum(acc, axis=0, keepdims=True)

    def bit_body(it, carry):
        thr, n_ge = carry
        cand = thr + lax.shift_left(jnp.int32(1), 31 - it)
        cnt = count(lambda k: k >= cand)
        ok = cnt >= topk
        return jnp.where(ok, cand, thr), jnp.where(ok, cnt, n_ge)

    n_all = (i + 1) * t
    passes = jnp.where(n_all > topk, 32, 0)
    thr, n_ge = lax.fori_loop(0, passes, bit_body, (jnp.full((1, t), INT_MIN, jnp.int32),
                                                    jnp.full((1, t), n_all, jnp.int32)))
    n_gt = count(lambda k: k > thr)
    ties_kept = (topk - n_gt).astype(F32)
    ties_all = (n_ge - n_gt).astype(F32)

    qa = jnp.concatenate(head_rows(sq_ref[...], DSA_HEADS), axis=0)
    earlier = jnp.where(qpos < kpos, 1.0, 0.0).astype(BF16)
    seen_ref[...] = jnp.zeros(seen_ref.shape, F32)

    surplus_ties = jnp.max(n_ge) > topk

    def masked_logits(ranked, j, diag):
        off = pl.multiple_of(j * t, t)
        key = key_ref[j]
        if ranked:
            eq = key == thr
            eqf = jnp.where(eq, 1.0, 0.0)
            ties_here = jnp.sum(eqf, axis=0, keepdims=True)
            if diag:
                seen = ties_all - ties_here
            else:
                seen = seen_ref[...]
                seen_ref[...] = seen + ties_here
            rank = _dot(earlier, eqf.astype(BF16)) + seen
            sel = (key > thr) | (eq & (rank < ties_kept))
        else:
            sel = key >= thr
        if diag:
            sel = sel & causal_t
        bias = jnp.where(sel, 0.0, NEG).T
        s = _dot_nt(qa, sk_ref[pl.ds(off, t), :])
        return jnp.concatenate([s[h * t:(h + 1) * t, :] + bias for h in range(DSA_HEADS)], axis=0)

    def pass1(ranked, js, diag):
        _store_logits([(j, functools.partial(masked_logits, ranked, j, diag)) for j in js],
                      s_ref, mx_ref, first=diag)

    pass1(True, [i], True)

    @pl.when(surplus_ties)
    def _():
        _for_blocks(i, lambda js: pass1(True, js, False))

    @pl.when(jnp.logical_not(surplus_ties))
    def _():
        _for_blocks(i, lambda js: pass1(False, js, False))
    _weighted_values(i, t, s_ref, mx_ref, acc_ref, sv_ref, 1)
    o = _normalised(acc_ref[...])
    for c in range(DSA_HEADS // 2):
        lo = o[2 * c * t:(2 * c + 1) * t]
        hi = pltpu.roll(o[(2 * c + 1) * t:(2 * c + 2) * t], HEAD_DIM, 1)
        o_ref[:, c * LANES:(c + 1) * LANES] = jnp.where(upper, hi, lo).astype(BF16)


def _dsa(main, gates, *, batch, seq, t, topk):
    nq = seq // t
    nsq = DSA_HEADS // 2
    niq = IDX_HEADS // 2
    return pl.pallas_call(
        functools.partial(_dsa_kernel, t=t, topk=topk),
        grid=(batch, nq),
        in_specs=[
            pl.BlockSpec((t, niq * LANES), lambda b, i: (b * nq + i, C_IQ // niq)),
            pl.BlockSpec((seq, LANES), lambda b, i: (b, C_IK)),
            pl.BlockSpec((t, LANES), lambda b, i: (b * nq + i, 0)),
            pl.BlockSpec((t, nsq * LANES), lambda b, i: (b * nq + i, C_SQ // nsq)),
            pl.BlockSpec((seq, LANES), lambda b, i: (b, C_SK)),
            pl.BlockSpec((seq, LANES), lambda b, i: (b, C_SVA)),
        ],
        out_specs=pl.BlockSpec((t, nsq * LANES), lambda b, i: (b * nq + i, 0)),
        out_shape=jax.ShapeDtypeStruct((batch * seq, nsq * LANES), BF16),
        scratch_shapes=[
            pltpu.VMEM((nq + 1, t, t), jnp.int32),
            pltpu.VMEM((nq, DSA_HEADS * t, t), F32),
            pltpu.VMEM((DSA_HEADS * t, LANES), F32),
            pltpu.VMEM((DSA_HEADS * t, LANES), F32),
            pltpu.VMEM((1, t), F32),
        ],
        compiler_params=pltpu.CompilerParams(vmem_limit_bytes=VMEM_LIMIT),
        name="dsa",
    )(main, main, gates, main, main, main)


def _ffn_kernel(x_ref, of_ref, od_ref, os_ref, wo_ref, gn_ref, wgu_ref, wd_ref, o_ref, acc_ref,
                *, chunk):
    mixed = jnp.concatenate([of_ref[...], od_ref[...], os_ref[...]], axis=1)
    x1 = x_ref[...] + _dot(mixed, wo_ref[...])
    ms = jnp.mean(x1 * x1, axis=-1, keepdims=True)
    h = (x1 * lax.rsqrt(ms + NORM_EPS) * gn_ref[...]).astype(BF16)
    acc_ref[...] = x1
    hidden = wd_ref.shape[0]
    for c0 in range(0, hidden, chunk):
        gate = _dot(h, wgu_ref[:, c0:c0 + chunk])
        up = _dot(h, wgu_ref[:, hidden + c0:hidden + c0 + chunk])
        a = (gate * jax.nn.sigmoid(gate) * up).astype(BF16)
        acc_ref[...] += _dot(a, wd_ref[c0:c0 + chunk, :])
    o_ref[...] = acc_ref[...]


def _ffn(x2, o_fox, o_diff, o_dsa, wo, gn, wgu, wd, *, layer, tm, chunk):
    rows, d = x2.shape

    def rowblk(a):
        return pl.BlockSpec((tm, a.shape[1]), lambda i: (i, 0))

    return pl.pallas_call(
        functools.partial(_ffn_kernel, chunk=chunk),
        grid=(rows // tm,),
        in_specs=[rowblk(x2), rowblk(o_fox), rowblk(o_diff), rowblk(o_dsa),
                  _layer_block(wo, layer), _layer_block(gn, layer), _layer_block(wgu, layer),
                  _layer_block(wd, layer)],
        out_specs=pl.BlockSpec((tm, d), lambda i: (i, 0)),
        out_shape=jax.ShapeDtypeStruct((rows, d), F32),
        scratch_shapes=[pltpu.VMEM((tm, d), F32)],
        compiler_params=pltpu.CompilerParams(vmem_limit_bytes=VMEM_LIMIT),
        name="ffn",
    )(x2, o_fox, o_diff, o_dsa, wo, gn, wgu, wd)


def _relayout_w_in(w):
    sizes = [FOX_HEADS * HEAD_DIM] * 3 + [FOX_HEADS] + [DIFF_HEADS * 2 * DIFF_QK_DIM] * 2 + \
            [DIFF_HEADS * DIFF_V_DIM, DSA_HEADS * HEAD_DIM, HEAD_DIM, HEAD_DIM,
             IDX_HEADS * HEAD_DIM, HEAD_DIM, IDX_HEADS]
    starts = np.concatenate([[0], np.cumsum(sizes)])
    fq, fk, fv, ff, dq, dk, dv, sq, sk, sv, iq, ik, iw = [w[..., int(starts[n]):int(starts[n + 1])]
                                                          for n in range(len(sizes))]
    pad = jnp.zeros(w.shape[:-1] + (LANES - FOX_HEADS - IDX_HEADS,), w.dtype)
    w_all = jnp.concatenate([fq, fk, fv, dq, dk, dv, sq, sk, sk, sv, sv, iq, ik, ik, ff, iw, pad],
                            axis=-1)
    assert w_all.shape[-1] == N_PROJ * LANES
    return w_all.astype(BF16)


def _rope_tables(seq, head_dim):
    rot = head_dim // 4
    half = rot // 2
    inv_freq = 1.0 / (ROPE_THETA ** (jnp.arange(0, rot, 2, dtype=F32) / rot))
    ang = jnp.arange(seq, dtype=F32)[:, None] * inv_freq[None, :]
    cos, sin = jnp.cos(ang), jnp.sin(ang)
    r = np.arange(LANES) % head_dim
    f = r % half
    cos_t = jnp.where(r < rot, cos[:, f], 1.0)
    sin_hi = jnp.where((r >= half) & (r < rot), sin[:, f], 0.0)
    sin_lo = jnp.where(r < half, -sin[:, f], 0.0)
    return [cos_t, sin_hi, sin_lo]


def _pad_lanes(v):
    return jnp.pad(v.astype(F32), ((0, 0), (0, LANES - v.shape[1])))


@jax.jit
def kernel(x, attn_norm, w_in, fox_fb, fox_qn, fox_kn, diff_qn, diff_kn, diff_lq1, diff_lk1,
           diff_lq2, diff_lk2, diff_subln, dsa_qn, dsa_kn, w_out, ffn_norm, w_gate_up, w_down):
    batch, seq, d = x.shape
    depth = w_in.shape[0]
    hidden = w_down.shape[1]
    topk = min(DSA_TOPK, seq // 4)
    t_att = 256
    tm = 512 if (batch * seq) % 512 == 0 else 256
    tm_in = min(tm, seq)
    ffn_chunk = 256
    assert hidden % ffn_chunk == 0 and seq % t_att == 0 and seq % tm_in == 0

    ropetab = jnp.stack(_rope_tables(seq, HEAD_DIM) + _rope_tables(seq, DIFF_QK_DIM))

    w_all = _relayout_w_in(w_in)
    wo = w_out.astype(BF16)
    wgu = w_gate_up.astype(BF16)
    wd = w_down.astype(BF16)
    gn_attn = attn_norm.astype(F32)[:, None, :]
    gn_ffn = ffn_norm.astype(F32)[:, None, :]
    vec = jnp.stack([
        jnp.tile(fox_qn.astype(F32), (1, 2)) * (HEAD_DIM ** -0.5 * LOG2E),
        jnp.tile(fox_kn.astype(F32), (1, 2)),
        jnp.tile(diff_qn.astype(F32), (1, 4)) * (DIFF_QK_DIM ** -0.5 * LOG2E),
        jnp.tile(diff_kn.astype(F32), (1, 4)),
        jnp.tile(dsa_qn.astype(F32), (1, 2)) * (HEAD_DIM ** -0.5 * LOG2E),
        jnp.tile(dsa_kn.astype(F32), (1, 2)),
        _pad_lanes(fox_fb),
        jnp.zeros((depth, LANES), F32),
    ], axis=1)
    lam_init = jnp.asarray([0.8 - 0.6 * math.exp(-0.3 * l) for l in range(depth)], F32)
    lam = (jnp.exp(jnp.sum(diff_lq1.astype(F32) * diff_lk1.astype(F32), axis=-1))
           - jnp.exp(jnp.sum(diff_lq2.astype(F32) * diff_lk2.astype(F32), axis=-1)) + lam_init)
    lam = lam[:, None]
    gsub = (jnp.tile(diff_subln.astype(F32), (1, 2)) * (1.0 - lam_init)[:, None])[:, None, :]

    x2 = x.reshape(batch * seq, d)
    for l in range(depth):
        main, gates = _inproj(x2, gn_attn, w_all, vec, ropetab, layer=l, seq=seq, tm=tm_in)
        ks, qs = _gates(gates, batch=batch, seq=seq, blk=t_att)
        o_fox = _fox(main, ks, qs, batch=batch, seq=seq, t=t_att)
        o_diff = _diff(lam, main, gsub, layer=l, batch=batch, seq=seq, t=t_att)
        o_dsa = _dsa(main, gates, batch=batch, seq=seq, t=t_att, topk=topk)
        x2 = _ffn(x2, o_fox, o_diff, o_dsa, wo, gn_ffn, wgu, wd, layer=l, tm=tm, chunk=ffn_chunk)
    return x2.reshape(batch, seq, d)
```

```python
import functools
import math

import numpy as np
import jax
import jax.numpy as jnp
from jax import lax
from jax.experimental import pallas as pl
from jax.experimental.pallas import tpu as pltpu

F32 = jnp.float32
BF16 = jnp.bfloat16
LANES = 128
SUBLANES = 8
NORM_EPS = 1e-6
NEG = -1e30
INT_MIN = -(2 ** 31)
ROPE_THETA = 500000.0
LOG2E = 1.4426950408889634

HEAD_DIM = 64
FOX_HEADS = 6
DIFF_HEADS = 4
DIFF_QK_DIM = 32
DIFF_V_DIM = 64
DSA_HEADS = 6
IDX_HEADS = 4
DSA_TOPK = 256

P_FQ, P_FK, P_FV = 0, 3, 6
P_DQ, P_DK, P_DV = 9, 11, 13
P_SQ, P_SK, P_SV = 15, 18, 19
P_IQ, P_IK, P_GATES = 20, 22, 23
N_PROJ = 24
C_FQ, C_FK, C_SQ, C_SK = 0, 3, 6, 9
C_IQ, C_DQ, C_DK = 10, 12, 14
C_SVA, C_IK = 16, 17
C_FVA, C_DVA = 18, 24
N_MAIN = 28
CS_LANE0, CT_LANE0 = 0, 32

VMEM_LIMIT = 56 * 1024 * 1024


def _dot(a, b):
    return jnp.dot(a, b, preferred_element_type=F32)


def _dot_nt(a, b):
    return lax.dot_general(a, b, (((1,), (1,)), ((), ())), preferred_element_type=F32)


def _layer_block(a, layer):
    nd = a.ndim - 1
    return pl.BlockSpec((None,) + a.shape[1:], lambda *_: (layer,) + (0,) * nd,
                        pipeline_mode=pl.Buffered(1))


def _block_diag_ones(seg):
    r = lax.broadcasted_iota(jnp.int32, (LANES, LANES), 0) // seg
    c = lax.broadcasted_iota(jnp.int32, (LANES, LANES), 1) // seg
    return jnp.where(r == c, 1.0, 0.0).astype(BF16)


def _split3(v):
    hi = v.astype(BF16)
    r1 = v - hi.astype(F32)
    mid = r1.astype(BF16)
    lo = (r1 - mid.astype(F32)).astype(BF16)
    return hi, mid, lo


def _seg_sum(v, bd):
    hi = v.astype(BF16)
    lo = (v - hi.astype(F32)).astype(BF16)
    return _dot(hi, bd) + _dot(lo, bd)


def _inproj_kernel(x_ref, gn_ref, w_ref, vec_ref, rope_ref, main_ref, gates_ref, p_ref):
    x = x_ref[...]
    ms = jnp.mean(x * x, axis=-1, keepdims=True)
    h = (x * lax.rsqrt(ms + NORM_EPS) * gn_ref[...]).astype(BF16)
    p_ref[...] = _dot(h, w_ref[...])

    lane = lax.broadcasted_iota(jnp.int32, (x.shape[0], LANES), 1)
    upper = lane >= HEAD_DIM

    def chunk(c):
        return p_ref[:, c * LANES:(c + 1) * LANES]

    def put(c, y):
        main_ref[:, c * LANES:(c + 1) * LANES] = y.astype(BF16)

    def put_v_ones(c, y):
        put(c, jnp.where(upper, 1.0, y))
        put(c + 1, jnp.where(upper, 1.0, pltpu.roll(y, HEAD_DIM, 1)))

    def rope(y, tabs):
        if tabs is None:
            return y
        t0, sh = tabs
        return (y * rope_ref[t0] + pltpu.roll(y, sh, 1) * rope_ref[t0 + 1]
                + pltpu.roll(y, LANES - sh, 1) * rope_ref[t0 + 2])

    rope64, rope32 = (0, 8), (3, 4)
    bd64 = _block_diag_ones(HEAD_DIM)
    bd32 = _block_diag_ones(DIFF_QK_DIM)

    def segnorm(y, bd, seg, row):
        return y * lax.rsqrt(_seg_sum(y * y, bd) * (1.0 / seg) + NORM_EPS) * vec_ref[row:row + 1, :]

    for c in range(3):
        put(C_FQ + c, segnorm(chunk(P_FQ + c), bd64, HEAD_DIM, 0))
        put(C_FK + c, segnorm(chunk(P_FK + c), bd64, HEAD_DIM, 1))
        put_v_ones(C_FVA + 2 * c, chunk(P_FV + c))
        put(C_SQ + c, rope(segnorm(chunk(P_SQ + c), bd64, HEAD_DIM, 4), rope64))
    for c in range(2):
        put(C_DQ + c, rope(segnorm(chunk(P_DQ + c), bd32, DIFF_QK_DIM, 2), rope32))
        put(C_DK + c, rope(segnorm(chunk(P_DK + c), bd32, DIFF_QK_DIM, 3), rope32))
        put_v_ones(C_DVA + 2 * c, chunk(P_DV + c))
        put(C_IQ + c, rope(chunk(P_IQ + c), rope64))
    put(C_SK, rope(segnorm(chunk(P_SK), bd64, HEAD_DIM, 5), rope64))
    put(C_SVA, jnp.where(upper, 1.0, chunk(P_SV)))
    put(C_IK, rope(chunk(P_IK), rope64))

    z = chunk(P_GATES)
    v = z + vec_ref[6:7, :]
    logsig = jnp.minimum(v, 0.0) - jnp.log1p(jnp.exp(-jnp.abs(v)))
    gates_ref[...] = jnp.where(lane < FOX_HEADS, logsig, z * 0.0625)


def _inproj(x2, gn, w_all, vec, ropetab, *, layer, seq, tm):
    rows = x2.shape[0]
    nst = seq // tm
    return pl.pallas_call(
        _inproj_kernel,
        grid=(rows // tm,),
        in_specs=[
            pl.BlockSpec((tm, x2.shape[1]), lambda i: (i, 0)),
            _layer_block(gn, layer),
            _layer_block(w_all, layer),
            _layer_block(vec, layer),
            pl.BlockSpec((6, tm, LANES), lambda i: (0, i % nst, 0)),
        ],
        out_specs=[
            pl.BlockSpec((tm, N_MAIN * LANES), lambda i: (i, 0)),
            pl.BlockSpec((tm, LANES), lambda i: (i, 0)),
        ],
        out_shape=[
            jax.ShapeDtypeStruct((rows, N_MAIN * LANES), BF16),
            jax.ShapeDtypeStruct((rows, LANES), F32),
        ],
        scratch_shapes=[pltpu.VMEM((tm, N_PROJ * LANES), F32)],
        compiler_params=pltpu.CompilerParams(vmem_limit_bytes=VMEM_LIMIT),
        name="inproj",
    )(x2, gn, w_all, vec, ropetab)


def _gates_kernel(g_ref, ks_ref, qs_ref, *, blk):
    n = g_ref.shape[0] // blk
    r = lax.broadcasted_iota(jnp.int32, (blk, blk), 0)
    c = lax.broadcasted_iota(jnp.int32, (blk, blk), 1)
    tri = jnp.where(r >= c, 1.0, 0.0).astype(BF16)
    er = lax.broadcasted_iota(jnp.int32, (LANES, LANES), 0)
    ec = lax.broadcasted_iota(jnp.int32, (LANES, LANES), 1)
    head = er < FOX_HEADS

    def place(lane0, part, val):
        return jnp.where(head & (ec == lane0 + 3 * er + part), val, 0.0).astype(BF16)

    lane = lax.broadcasted_iota(jnp.int32, (blk, LANES), 1)
    k_ones = jnp.where((lane >= CT_LANE0) & (lane < CT_LANE0 + 3 * FOX_HEADS), 1.0, 0.0)
    carry = jnp.zeros((1, LANES), F32)
    for b in range(n):
        parts = _split3(g_ref[b * blk:(b + 1) * blk, :])
        cb = _dot(tri, parts[0]) + _dot(tri, parts[1]) + _dot(tri, parts[2]) + carry
        carry = cb[blk - 1:blk, :]
        c2 = _split3(cb * LOG2E)
        ks = k_ones
        qs = jnp.zeros((blk, LANES), F32)
        for part in range(3):
            ks = ks + _dot(c2[part], place(CS_LANE0, part, -1.0))
            qs = qs + _dot(c2[part], place(CT_LANE0, part, 1.0))
        ks_ref[b * blk:(b + 1) * blk, :] = ks.astype(BF16)
        qs_ref[b * blk:(b + 1) * blk, :] = qs.astype(BF16)


def _gates(gates, *, batch, seq, blk):
    return pl.pallas_call(
        functools.partial(_gates_kernel, blk=blk),
        grid=(batch,),
        in_specs=[pl.BlockSpec((seq, LANES), lambda b: (b, 0))],
        out_specs=[pl.BlockSpec((seq, LANES), lambda b: (b, 0)),
                   pl.BlockSpec((seq, LANES), lambda b: (b, 0))],
        out_shape=[jax.ShapeDtypeStruct((batch * seq, LANES), BF16),
                   jax.ShapeDtypeStruct((batch * seq, LANES), BF16)],
        name="gates",
    )(gates)


def _causal_rows(n, t):
    assert t & (t - 1) == 0
    row = lax.broadcasted_iota(jnp.int32, (n * t, t), 0) & (t - 1)
    col = lax.broadcasted_iota(jnp.int32, (n * t, t), 1)
    return col <= row


def _store_logits(blocks, s_ref, mx_ref, first=False):
    m = None if first else mx_ref[...]
    for j, logits in blocks:
        s = logits()
        s_ref[j] = s
        for c in range(s.shape[1] // LANES):
            sc = s[:, c * LANES:(c + 1) * LANES]
            m = sc if m is None else jnp.maximum(m, sc)
    mx_ref[...] = m


def _blocks_diag_first(i, fn):
    fn([i], True)
    _for_blocks(i, lambda js: fn(js, False))


def _causal_pass1(logits, causal, s_ref, mx_ref, js, diag):
    blocks = [(j, functools.partial(logits, j)) for j in js]
    if diag:
        j = js[-1]
        blocks[-1] = (j, lambda: jnp.where(causal, logits(j), NEG))
    _store_logits(blocks, s_ref, mx_ref, first=diag)


def _for_blocks(n, fn):
    def body(jj, carry):
        fn([4 * jj + u for u in range(4)])
        return carry

    lax.fori_loop(0, n // 4, body, 0)
    done4 = (n // 4) * 4

    @pl.when(n % 4 >= 2)
    def _():
        fn([done4, done4 + 1])

    @pl.when(n % 2 == 1)
    def _():
        fn([n - 1])


def _weighted_values(i, t, s_ref, mx_ref, acc_ref, v_ref, groups):
    m = jnp.max(mx_ref[...], axis=1, keepdims=True)
    rows = acc_ref.shape[0] // groups

    def pv(j):
        off = pl.multiple_of(j * t, t)
        p = jnp.exp2(s_ref[j] - m).astype(BF16)
        return [_dot(p[g * rows:(g + 1) * rows, :], v_ref[pl.ds(off, t), g * LANES:(g + 1) * LANES])
                for g in range(groups)]

    def add(js, first):
        parts = [pv(j) for j in js]
        for g in range(groups):
            total = functools.reduce(lambda a, b: a + b, [p[g] for p in parts])
            if first:
                acc_ref[g * rows:(g + 1) * rows, :] = total
            else:
                acc_ref[g * rows:(g + 1) * rows, :] += total

    _blocks_diag_first(i, add)


def _normalised(acc):
    return acc / pltpu.roll(acc, HEAD_DIM, 1)


def _fox_kernel(q_ref, qs_ref, k_ref, ks_ref, v_ref, o_ref, s_ref, mx_ref, acc_ref, *, t):
    i = pl.program_id(1)
    npair = FOX_HEADS // 2
    qs = qs_ref[...].astype(F32)
    lane = lax.broadcasted_iota(jnp.int32, (t, LANES), 1)
    upper = lane >= HEAD_DIM
    zero = jnp.zeros((t, LANES), BF16)

    def side(h):
        cs0 = CS_LANE0 + 3 * h
        ct0 = CT_LANE0 + 3 * h
        pick = jnp.where((lane >= cs0) & (lane < cs0 + 3), 1.0, 0.0)
        return jnp.where((lane >= ct0) & (lane < ct0 + 3), qs, pick).astype(BF16)

    qa = []
    for p in range(npair):
        q = q_ref[:, p * LANES:(p + 1) * LANES]
        qa.append(jnp.concatenate([
            jnp.concatenate([jnp.where(upper, zero, q), side(2 * p)], axis=1),
            jnp.concatenate([jnp.where(upper, q, zero), side(2 * p + 1)], axis=1)], axis=0))

    def logits(j):
        off = pl.multiple_of(j * t, t)
        ksb = ks_ref[pl.ds(off, t), :]
        return jnp.concatenate([
            _dot_nt(qa[p], jnp.concatenate([k_ref[pl.ds(off, t), p * LANES:(p + 1) * LANES], ksb],
                                           axis=1))
            for p in range(npair)], axis=0)

    _blocks_diag_first(i, functools.partial(_causal_pass1, logits, _causal_rows(FOX_HEADS, t),
                                            s_ref, mx_ref))
    _weighted_values(i, t, s_ref, mx_ref, acc_ref, v_ref, FOX_HEADS)
    o = _normalised(acc_ref[...])
    for p in range(npair):
        lo = o[2 * p * t:(2 * p + 1) * t]
        hi = pltpu.roll(o[(2 * p + 1) * t:(2 * p + 2) * t], HEAD_DIM, 1)
        o_ref[:, p * LANES:(p + 1) * LANES] = jnp.where(upper, hi, lo).astype(BF16)


def _fox(main, ks, qs, *, batch, seq, t):
    nq = seq // t
    npair = FOX_HEADS // 2
    return pl.pallas_call(
        functools.partial(_fox_kernel, t=t),
        grid=(batch, nq),
        in_specs=[
            pl.BlockSpec((t, npair * LANES), lambda b, i: (b * nq + i, C_FQ // npair)),
            pl.BlockSpec((t, LANES), lambda b, i: (b * nq + i, 0)),
            pl.BlockSpec((seq, npair * LANES), lambda b, i: (b, C_FK // npair)),
            pl.BlockSpec((seq, LANES), lambda b, i: (b, 0)),
            pl.BlockSpec((seq, FOX_HEADS * LANES), lambda b, i: (b, C_FVA // FOX_HEADS)),
        ],
        out_specs=pl.BlockSpec((t, npair * LANES), lambda b, i: (b * nq + i, 0)),
        out_shape=jax.ShapeDtypeStruct((batch * seq, npair * LANES), BF16),
        scratch_shapes=[
            pltpu.VMEM((nq, FOX_HEADS * t, t), F32),
            pltpu.VMEM((FOX_HEADS * t, LANES), F32),
            pltpu.VMEM((FOX_HEADS * t, LANES), F32),
        ],
        compiler_params=pltpu.CompilerParams(vmem_limit_bytes=VMEM_LIMIT),
        name="fox",
    )(main, qs, main, ks, main)


def _diff_kernel(lam_ref, q_ref, k_ref, v_ref, g_ref, o_ref, s_ref, mx_ref, acc_ref, *, t, layer):
    i = pl.program_id(1)
    npair = DIFF_HEADS // 2
    nmaps = 2 * DIFF_HEADS
    lane = lax.broadcasted_iota(jnp.int32, (t, LANES), 1)
    upper = lane >= DIFF_V_DIM
    zero = jnp.zeros((t, LANES), BF16)
    qa = []
    for p in range(npair):
        q = q_ref[:, p * LANES:(p + 1) * LANES]
        qa.append(jnp.concatenate([jnp.where((lane // DIFF_QK_DIM) == n, q, zero) for n in range(4)],
                                  axis=0))

    def logits(j):
        off = pl.multiple_of(j * t, t)
        return jnp.concatenate([_dot_nt(qa[p], k_ref[pl.ds(off, t), p * LANES:(p + 1) * LANES])
                                for p in range(npair)], axis=0)

    _blocks_diag_first(i, functools.partial(_causal_pass1, logits, _causal_rows(nmaps, t),
                                            s_ref, mx_ref))
    _weighted_values(i, t, s_ref, mx_ref, acc_ref, v_ref, DIFF_HEADS)
    o = _normalised(acc_ref[...])
    lam = lam_ref[layer, 0]
    bd = _block_diag_ones(DIFF_V_DIM)
    for p in range(npair):
        heads = []
        for h in (2 * p, 2 * p + 1):
            heads.append(o[2 * h * t:(2 * h + 1) * t] - lam * o[(2 * h + 1) * t:(2 * h + 2) * t])
        y = jnp.where(upper, pltpu.roll(heads[1], DIFF_V_DIM, 1), heads[0])
        ss = _seg_sum(y * y, bd)
        y = y * lax.rsqrt(ss * (1.0 / DIFF_V_DIM) + NORM_EPS) * g_ref[...]
        o_ref[:, p * LANES:(p + 1) * LANES] = y.astype(BF16)


def _diff(lam, main, gsub, *, layer, batch, seq, t):
    nq = seq // t
    npair = DIFF_HEADS // 2
    nmaps = 2 * DIFF_HEADS
    return pl.pallas_call(
        functools.partial(_diff_kernel, t=t, layer=layer),
        grid=(batch, nq),
        in_specs=[
            pl.BlockSpec(memory_space=pltpu.SMEM),
            pl.BlockSpec((t, npair * LANES), lambda b, i: (b * nq + i, C_DQ // npair)),
            pl.BlockSpec((seq, npair * LANES), lambda b, i: (b, C_DK // npair)),
            pl.BlockSpec((seq, DIFF_HEADS * LANES), lambda b, i: (b, C_DVA // DIFF_HEADS)),
            _layer_block(gsub, layer),
        ],
        out_specs=pl.BlockSpec((t, npair * LANES), lambda b, i: (b * nq + i, 0)),
        out_shape=jax.ShapeDtypeStruct((batch * seq, npair * LANES), BF16),
        scratch_shapes=[
            pltpu.VMEM((nq, nmaps * t, t), F32),
            pltpu.VMEM((nmaps * t, LANES), F32),
            pltpu.VMEM((nmaps * t, LANES), F32),
        ],
        compiler_params=pltpu.CompilerParams(vmem_limit_bytes=VMEM_LIMIT),
        name="diff",
    )(lam, main, main, main, gsub)


def _dsa_kernel(iq_ref, ik_ref, g_ref, sq_ref, sk_ref, sv_ref, o_ref,
                key_ref, khi_ref, klo_ref, s_ref, mx_ref, acc_ref, seen_ref, *, t, topk):
    i = pl.program_id(1)
    lane = lax.broadcasted_iota(jnp.int32, (t, LANES), 1)
    upper = lane >= HEAD_DIM
    zero = jnp.zeros((t, LANES), BF16)
    kpos = lax.broadcasted_iota(jnp.int32, (t, t), 0)
    qpos = lax.broadcasted_iota(jnp.int32, (t, t), 1)
    causal_t = kpos <= qpos

    def head_rows(x, nheads):
        out = []
        for h in range(nheads):
            xc = x[:, (h // 2) * LANES:(h // 2 + 1) * LANES]
            out.append(jnp.where(upper, xc, zero) if h % 2 else jnp.where(upper, zero, xc))
        return out

    iqm = head_rows(iq_ref[...], IDX_HEADS)
    g_t = g_ref[...].T
    wts = [g_t[FOX_HEADS + h:FOX_HEADS + h + 1, :] for h in range(IDX_HEADS)]

    def score_block(j, diag):
        off = pl.multiple_of(j * t, t)
        kb = ik_ref[pl.ds(off, t), :]
        sc = wts[0] * jnp.maximum(_dot_nt(kb, iqm[0]), 0.0)
        for h in range(1, IDX_HEADS):
            sc = sc + wts[h] * jnp.maximum(_dot_nt(kb, iqm[h]), 0.0)
        bits = lax.bitcast_convert_type(sc, jnp.int32)
        key = jnp.where(bits < 0, jnp.int32(INT_MIN) - bits, bits)
        if diag:
            key = jnp.where(causal_t, key, INT_MIN)
        key_ref[j] = key
        khi_ref[j] = lax.shift_right_arithmetic(key, 16).astype(jnp.int16)

    _for_blocks(i, lambda js: [score_block(j, False) for j in js])
    score_block(i, True)

    half = t // 2

    def count(strictly, cand, src_ref):
        dtype = src_ref.dtype
        acc_rows = 4 * SUBLANES * (4 // dtype.itemsize)

        def bump(key, cnd, acc):
            pred = (key > cnd) if strictly else (key >= cnd)
            for g in range(key.shape[0] // acc_rows):
                acc = jnp.where(pred[g * acc_rows:(g + 1) * acc_rows, :], acc + 1, acc)
            return acc

        def cbody(jj, acc):
            return bump(src_ref[2 * jj + 1], cand, bump(src_ref[2 * jj], cand, acc))

        acc = lax.fori_loop(0, i // 2, cbody, jnp.zeros((acc_rows, t), dtype))
        acc = lax.cond(i % 2 == 1, lambda a: bump(src_ref[i - 1], cand, a), lambda a: a, acc)
        acc = bump(src_ref[i, 0:half, :], cand, acc)
        later = bump(src_ref[i, half:t, half:t], cand[:, half:t], acc[:, half:t])
        acc = jnp.concatenate([acc[:, 0:half], later], axis=1)
        return jnp.sum(acc.astype(jnp.int32), axis=0, keepdims=True)

    def top_bits(it, carry):
        thr, n_ge = carry
        cand = thr + lax.shift_left(jnp.int32(1), 31 - it)
        cnt = count(False, lax.shift_right_arithmetic(cand, 16).astype(jnp.int16), khi_ref)
        ok = cnt >= topk
        return jnp.where(ok, cand, thr), jnp.where(ok, cnt, n_ge)

    n_all = (i + 1) * t
    passes = jnp.where(n_all > topk, 16, 0)
    thr_top, n_ge = lax.fori_loop(0, passes, top_bits, (jnp.full((1, t), INT_MIN, jnp.int32),
                                                        jnp.full((1, t), n_all, jnp.int32)))

    top16 = lax.shift_right_arithmetic(thr_top, 16).astype(jnp.int16)
    n_above = count(True, top16, khi_ref)

    def pack_low(j, carry):
        low = ((key_ref[j] & 0xFFFF) - 0x8000).astype(jnp.int16)
        klo_ref[j] = jnp.where(khi_ref[j] == top16, low, jnp.int16(-0x8000))
        return carry

    lax.fori_loop(0, i + 1, pack_low, 0)

    def low_bits(it, carry):
        low, n_ge = carry
        cand = low + lax.shift_left(jnp.int32(1), 15 - it)
        cnt = n_above + count(False, (cand - 0x8000).astype(jnp.int16), klo_ref)
        ok = cnt >= topk
        return jnp.where(ok, cand, low), jnp.where(ok, cnt, n_ge)

    thr_low, n_ge = lax.fori_loop(0, passes, low_bits, (jnp.zeros((1, t), jnp.int32), n_ge))
    thr = thr_top + thr_low
    n_gt = n_above + count(True, (thr_low - 0x8000).astype(jnp.int16), klo_ref)
    ties_kept = (topk - n_gt).astype(F32)
    ties_all = (n_ge - n_gt).astype(F32)

    qa = jnp.concatenate(head_rows(sq_ref[...], DSA_HEADS), axis=0)
    earlier = jnp.where(qpos < kpos, 1.0, 0.0).astype(BF16)
    seen_ref[...] = jnp.zeros(seen_ref.shape, F32)

    surplus_ties = jnp.max(n_ge) > topk

    def masked_logits(ranked, j, diag):
        off = pl.multiple_of(j * t, t)
        key = key_ref[j]
        if ranked:
            eq = key == thr
            eqf = jnp.where(eq, 1.0, 0.0)
            ties_here = jnp.sum(eqf, axis=0, keepdims=True)
            if diag:
                seen = ties_all - ties_here
            else:
                seen = seen_ref[...]
                seen_ref[...] = seen + ties_here
            rank = _dot(earlier, eqf.astype(BF16)) + seen
            sel = (key > thr) | (eq & (rank < ties_kept))
        else:
            sel = key >= thr
        if diag:
            sel = sel & causal_t
        bias = jnp.where(sel, 0.0, NEG).T
        s = _dot_nt(qa, sk_ref[pl.ds(off, t), :])
        return jnp.concatenate([s[h * t:(h + 1) * t, :] + bias for h in range(DSA_HEADS)], axis=0)

    def pass1(ranked, js, diag):
        _store_logits([(j, functools.partial(masked_logits, ranked, j, diag)) for j in js],
                      s_ref, mx_ref, first=diag)

    pass1(True, [i], True)

    @pl.when(surplus_ties)
    def _():
        _for_blocks(i, lambda js: pass1(True, js, False))

    @pl.when(jnp.logical_not(surplus_ties))
    def _():
        _for_blocks(i, lambda js: pass1(False, js, False))
    _weighted_values(i, t, s_ref, mx_ref, acc_ref, sv_ref, 1)
    o = _normalised(acc_ref[...])
    for c in range(DSA_HEADS // 2):
        lo = o[2 * c * t:(2 * c + 1) * t]
        hi = pltpu.roll(o[(2 * c + 1) * t:(2 * c + 2) * t], HEAD_DIM, 1)
        o_ref[:, c * LANES:(c + 1) * LANES] = jnp.where(upper, hi, lo).astype(BF16)


def _dsa(main, gates, *, batch, seq, t, topk):
    nq = seq // t
    nsq = DSA_HEADS // 2
    niq = IDX_HEADS // 2
    return pl.pallas_call(
        functools.partial(_dsa_kernel, t=t, topk=topk),
        grid=(batch, nq),
        in_specs=[
            pl.BlockSpec((t, niq * LANES), lambda b, i: (b * nq + i, C_IQ // niq)),
            pl.BlockSpec((seq, LANES), lambda b, i: (b, C_IK)),
            pl.BlockSpec((t, LANES), lambda b, i: (b * nq + i, 0)),
            pl.BlockSpec((t, nsq * LANES), lambda b, i: (b * nq + i, C_SQ // nsq)),
            pl.BlockSpec((seq, LANES), lambda b, i: (b, C_SK)),
            pl.BlockSpec((seq, LANES), lambda b, i: (b, C_SVA)),
        ],
        out_specs=pl.BlockSpec((t, nsq * LANES), lambda b, i: (b * nq + i, 0)),
        out_shape=jax.ShapeDtypeStruct((batch * seq, nsq * LANES), BF16),
        scratch_shapes=[
            pltpu.VMEM((nq, t, t), jnp.int32),
            pltpu.VMEM((nq, t, t), jnp.int16),
            pltpu.VMEM((nq, t, t), jnp.int16),
            pltpu.VMEM((nq, DSA_HEADS * t, t), F32),
            pltpu.VMEM((DSA_HEADS * t, LANES), F32),
            pltpu.VMEM((DSA_HEADS * t, LANES), F32),
            pltpu.VMEM((1, t), F32),
        ],
        compiler_params=pltpu.CompilerParams(vmem_limit_bytes=VMEM_LIMIT),
        name="dsa",
    )(main, main, gates, main, main, main)


def _ffn_kernel(x_ref, of_ref, od_ref, os_ref, wo_ref, gn_ref, wgu_ref, wd_ref, o_ref, acc_ref,
                *, chunk):
    mixed = jnp.concatenate([of_ref[...], od_ref[...], os_ref[...]], axis=1)
    x1 = x_ref[...] + _dot(mixed, wo_ref[...])
    ms = jnp.mean(x1 * x1, axis=-1, keepdims=True)
    h = (x1 * lax.rsqrt(ms + NORM_EPS) * gn_ref[...]).astype(BF16)
    acc_ref[...] = x1
    hidden = wd_ref.shape[0]
    for c0 in range(0, hidden, chunk):
        gate = _dot(h, wgu_ref[:, c0:c0 + chunk])
        up = _dot(h, wgu_ref[:, hidden + c0:hidden + c0 + chunk])
        a = (gate * jax.nn.sigmoid(gate) * up).astype(BF16)
        acc_ref[...] += _dot(a, wd_ref[c0:c0 + chunk, :])
    o_ref[...] = acc_ref[...]


def _ffn(x2, o_fox, o_diff, o_dsa, wo, gn, wgu, wd, *, layer, tm, chunk):
    rows, d = x2.shape

    def rowblk(a):
        return pl.BlockSpec((tm, a.shape[1]), lambda i: (i, 0))

    return pl.pallas_call(
        functools.partial(_ffn_kernel, chunk=chunk),
        grid=(rows // tm,),
        in_specs=[rowblk(x2), rowblk(o_fox), rowblk(o_diff), rowblk(o_dsa),
                  _layer_block(wo, layer), _layer_block(gn, layer), _layer_block(wgu, layer),
                  _layer_block(wd, layer)],
        out_specs=pl.BlockSpec((tm, d), lambda i: (i, 0)),
        out_shape=jax.ShapeDtypeStruct((rows, d), F32),
        scratch_shapes=[pltpu.VMEM((tm, d), F32)],
        compiler_params=pltpu.CompilerParams(vmem_limit_bytes=VMEM_LIMIT),
        name="ffn",
    )(x2, o_fox, o_diff, o_dsa, wo, gn, wgu, wd)


def _relayout_w_in(w):
    sizes = [FOX_HEADS * HEAD_DIM] * 3 + [FOX_HEADS] + [DIFF_HEADS * 2 * DIFF_QK_DIM] * 2 + \
            [DIFF_HEADS * DIFF_V_DIM, DSA_HEADS * HEAD_DIM, HEAD_DIM, HEAD_DIM,
             IDX_HEADS * HEAD_DIM, HEAD_DIM, IDX_HEADS]
    starts = np.concatenate([[0], np.cumsum(sizes)])
    fq, fk, fv, ff, dq, dk, dv, sq, sk, sv, iq, ik, iw = [w[..., int(starts[n]):int(starts[n + 1])]
                                                          for n in range(len(sizes))]
    pad = jnp.zeros(w.shape[:-1] + (LANES - FOX_HEADS - IDX_HEADS,), w.dtype)
    w_all = jnp.concatenate([fq, fk, fv, dq, dk, dv, sq, sk, sk, sv, sv, iq, ik, ik, ff, iw, pad],
                            axis=-1)
    assert w_all.shape[-1] == N_PROJ * LANES
    return w_all.astype(BF16)


def _rope_tables(seq, head_dim):
    rot = head_dim // 4
    half = rot // 2
    inv_freq = 1.0 / (ROPE_THETA ** (jnp.arange(0, rot, 2, dtype=F32) / rot))
    ang = jnp.arange(seq, dtype=F32)[:, None] * inv_freq[None, :]
    cos, sin = jnp.cos(ang), jnp.sin(ang)
    r = np.arange(LANES) % head_dim
    f = r % half
    cos_t = jnp.where(r < rot, cos[:, f], 1.0)
    sin_hi = jnp.where((r >= half) & (r < rot), sin[:, f], 0.0)
    sin_lo = jnp.where(r < half, -sin[:, f], 0.0)
    return [cos_t, sin_hi, sin_lo]


def _pad_lanes(v):
    return jnp.pad(v.astype(F32), ((0, 0), (0, LANES - v.shape[1])))


@jax.jit
def kernel(x, attn_norm, w_in, fox_fb, fox_qn, fox_kn, diff_qn, diff_kn, diff_lq1, diff_lk1,
           diff_lq2, diff_lk2, diff_subln, dsa_qn, dsa_kn, w_out, ffn_norm, w_gate_up, w_down):
    batch, seq, d = x.shape
    depth = w_in.shape[0]
    hidden = w_down.shape[1]
    topk = min(DSA_TOPK, seq // 4)
    t_att = 256
    tm = 512 if (batch * seq) % 512 == 0 else 256
    tm_in = min(tm, seq)
    ffn_chunk = 256
    assert hidden % ffn_chunk == 0 and seq % t_att == 0 and seq % tm_in == 0

    ropetab = jnp.stack(_rope_tables(seq, HEAD_DIM) + _rope_tables(seq, DIFF_QK_DIM))

    w_all = _relayout_w_in(w_in)
    wo = w_out.astype(BF16)
    wgu = w_gate_up.astype(BF16)
    wd = w_down.astype(BF16)
    gn_attn = attn_norm.astype(F32)[:, None, :]
    gn_ffn = ffn_norm.astype(F32)[:, None, :]
    vec = jnp.stack([
        jnp.tile(fox_qn.astype(F32), (1, 2)) * (HEAD_DIM ** -0.5 * LOG2E),
        jnp.tile(fox_kn.astype(F32), (1, 2)),
        jnp.tile(diff_qn.astype(F32), (1, 4)) * (DIFF_QK_DIM ** -0.5 * LOG2E),
        jnp.tile(diff_kn.astype(F32), (1, 4)),
        jnp.tile(dsa_qn.astype(F32), (1, 2)) * (HEAD_DIM ** -0.5 * LOG2E),
        jnp.tile(dsa_kn.astype(F32), (1, 2)),
        _pad_lanes(fox_fb),
        jnp.zeros((depth, LANES), F32),
    ], axis=1)
    lam_init = jnp.asarray([0.8 - 0.6 * math.exp(-0.3 * l) for l in range(depth)], F32)
    lam = (jnp.exp(jnp.sum(diff_lq1.astype(F32) * diff_lk1.astype(F32), axis=-1))
           - jnp.exp(jnp.sum(diff_lq2.astype(F32) * diff_lk2.astype(F32), axis=-1)) + lam_init)
    lam = lam[:, None]
    gsub = (jnp.tile(diff_subln.astype(F32), (1, 2)) * (1.0 - lam_init)[:, None])[:, None, :]

    x2 = x.reshape(batch * seq, d)
    for l in range(depth):
        main, gates = _inproj(x2, gn_attn, w_all, vec, ropetab, layer=l, seq=seq, tm=tm_in)
        ks, qs = _gates(gates, batch=batch, seq=seq, blk=t_att)
        o_fox = _fox(main, ks, qs, batch=batch, seq=seq, t=t_att)
        o_diff = _diff(lam, main, gsub, layer=l, batch=batch, seq=seq, t=t_att)
        o_dsa = _dsa(main, gates, batch=batch, seq=seq, t=t_att, topk=topk)
        x2 = _ffn(x2, o_fox, o_diff, o_dsa, wo, gn_ffn, wgu, wd, layer=l, tm=tm, chunk=ffn_chunk)
    return x2.reshape(batch, seq, d)
```
